```python
import math
import numpy as np
import jax
import jax.numpy as jnp
from jax import lax

D_MODEL = 2048
BATCH = 1
SEQ = 16384
DEPTH = 2

HEAD_DIM = 128
MIX_HEADS = D_MODEL // HEAD_DIM
MIX_W = MIX_HEADS * HEAD_DIM
QBLOCK = 128
NORM_EPS = 1e-6
RET_HEADS = MIX_HEADS // 2
FOX_HEADS = MIX_HEADS - RET_HEADS
RET_W = RET_HEADS * HEAD_DIM
FOX_W = FOX_HEADS * HEAD_DIM
RET_CHUNK = 128
FOX_GATE_BIAS = 3.0
DIFF_HEADS = MIX_HEADS // 2
NSA_HEADS = MIX_HEADS - DIFF_HEADS
DIFF_SUB = HEAD_DIM // 2
DIFF_W = DIFF_HEADS * HEAD_DIM
NSA_W = NSA_HEADS * HEAD_DIM
NSA_KV_HEADS = 2
NSA_GROUP = NSA_HEADS // NSA_KV_HEADS
NSA_KV_W = NSA_KV_HEADS * HEAD_DIM
CMP_LEN = 32
CMP_STRIDE = 16
SLC_LEN = 64
SLC_TOPN = 16
WINDOW = 512
SEL_BIG = 1e9
T5_BUCKETS = 32
T5_MAX_DIST = 128
T5_HEADS = DIFF_HEADS + NSA_HEADS
D_FF = -(-(8 * D_MODEL) // (3 * 256)) * 256
PLE_DIM = 256
N_EVEN = (DEPTH + 1) // 2
N_ODD = DEPTH // 2
EVEN_SPLITS = (RET_W, RET_W, RET_W, RET_W, FOX_W, FOX_W, FOX_W, FOX_HEADS)
ODD_SPLITS = (DIFF_W, DIFF_W, DIFF_W, NSA_W, NSA_KV_W, NSA_KV_W, NSA_KV_W, NSA_KV_W, NSA_KV_W, NSA_KV_W, 3 * NSA_HEADS)
EVEN_COLS = sum(EVEN_SPLITS)
ODD_COLS = sum(ODD_SPLITS)

kernel_name = 'hybrid_retention_fox_diff_nsa_trunk'


def rmsnorm(x, g):
    xf = x.astype(jnp.float32)
    y = xf * lax.rsqrt(jnp.mean(jnp.square(xf), -1, keepdims=True) + NORM_EPS) * g.astype(jnp.float32)
    return y.astype(x.dtype)


def split_cols(z, sizes):
    return jnp.split(z, [int(v) for v in np.cumsum(sizes)[:-1]], axis=-1)


def heads(a, n):
    B, S, _ = a.shape
    return a.reshape(B, S, n, -1).transpose(0, 2, 1, 3)


def merge(a):
    B, n, S, d = a.shape
    return a.transpose(0, 2, 1, 3).reshape(B, S, n * d)


def to_blocks(a, axis):
    shp = a.shape
    a = a.reshape(shp[:axis] + (shp[axis] // QBLOCK, QBLOCK) + shp[axis + 1:])
    return jnp.moveaxis(a, axis, 0)


def from_blocks(a, axis):
    a = jnp.moveaxis(a, 0, axis)
    shp = a.shape
    return a.reshape(shp[:axis] + (shp[axis] * shp[axis + 1],) + shp[axis + 2:])


def masked_softmax(logits, mask):
    logits = jnp.where(mask, logits.astype(jnp.float32), -jnp.inf)
    m = jnp.max(logits, -1, keepdims=True)
    m = jnp.where(jnp.isfinite(m), m, 0.0)
    e = jnp.where(mask, jnp.exp(logits - m), 0.0)
    return e / jnp.maximum(jnp.sum(e, -1, keepdims=True), 1e-30)


def t5_bucket(dist):
    n = jnp.maximum(dist, 0)
    max_exact = T5_BUCKETS // 2
    nf = jnp.maximum(n, 1).astype(jnp.float32)
    large = max_exact + (jnp.log(nf / max_exact) / math.log(T5_MAX_DIST / max_exact) * (T5_BUCKETS - max_exact)).astype(jnp.int32)
    large = jnp.minimum(large, T5_BUCKETS - 1)
    return jnp.where(n < max_exact, n, large)


def retention(q, k, v):
    B, H, S, d = q.shape
    C = RET_CHUNK
    N = S // C
    log_gamma = jnp.log1p(-jnp.exp2(-5.0 - jnp.arange(H, dtype=jnp.float32)))
    pos = jnp.arange(C, dtype=jnp.float32)
    rel = pos[:, None] - pos[None, :]
    intra_decay = jnp.where(rel >= 0, jnp.exp(log_gamma[:, None, None] * jnp.maximum(rel, 0.0)), 0.0)
    qc = q.reshape(B, H, N, C, d)
    kc = (k * d ** -0.5).reshape(B, H, N, C, d)
    vc = v.reshape(B, H, N, C, d)
    att = jnp.einsum('bhncd,bhnsd->bhncs', qc, kc) * intra_decay[None, :, None]
    y_intra = jnp.einsum('bhncs,bhnse->bhnce', att, vc)
    k_w = jnp.exp(log_gamma[:, None] * (C - 1 - pos))
    upd = jnp.einsum('bhnsd,bhnse->bhnde', kc * k_w[None, :, None, :, None], vc)
    chunk_decay = jnp.exp(log_gamma * C)[None, :, None, None]

    def step(state, u):
        return state * chunk_decay + u, state

    _, prev = lax.scan(step, jnp.zeros((B, H, d, d), jnp.float32), jnp.moveaxis(upd, 2, 0))
    prev = jnp.moveaxis(prev, 0, 2)
    q_w = jnp.exp(log_gamma[:, None] * (pos + 1.0))
    y_cross = jnp.einsum('bhncd,bhnde->bhnce', qc * q_w[None, :, None, :, None], prev)
    return (y_intra + y_cross).reshape(B, H, S, d)


def forgetting_attention(q, k, v, log_f):
    B, H, S, d = q.shape
    NB = S // QBLOCK
    c = jnp.cumsum(log_f, axis=-1)
    kpos = jnp.arange(S)
    scale = d ** -0.5

    def blk(args):
        qb, cb, i = args
        t = i * QBLOCK + jnp.arange(QBLOCK)
        logits = jnp.einsum('bhqd,bhsd->bhqs', qb, k).astype(jnp.float32) * scale + cb[..., None] - c[:, :, None, :]
        pr = masked_softmax(logits, kpos[None, :] <= t[:, None])
        return jnp.einsum('bhqs,bhsd->bhqd', pr.astype(v.dtype), v)

    out = lax.map(blk, (to_blocks(q, 2), to_blocks(c, 2), jnp.arange(NB)))
    return from_blocks(out, 2)


def diff_attention(q, k, v, lam, subln_g, lambda_init, t5_diff):
    B, H, S, _, e = q.shape
    NB = S // QBLOCK
    kpos = jnp.arange(S)
    lam_f = lam.astype(jnp.float32)
    lmbda = jnp.exp(jnp.sum(lam_f[0] * lam_f[1])) - jnp.exp(jnp.sum(lam_f[2] * lam_f[3])) + lambda_init
    table = t5_diff.astype(jnp.float32)
    scale = e ** -0.5

    def blk(args):
        qb, i = args
        t = i * QBLOCK + jnp.arange(QBLOCK)
        bias = jnp.transpose(table[t5_bucket(t[:, None] - kpos[None, :])], (2, 0, 1))
        logits = jnp.einsum('bhqme,bhsme->bhmqs', qb, k).astype(jnp.float32) * scale + bias[None, :, None]
        pr = masked_softmax(logits, kpos[None, :] <= t[:, None])
        w = pr[:, :, 0] - lmbda * pr[:, :, 1]
        return jnp.einsum('bhqs,bhsd->bhqd', w.astype(v.dtype), v)

    o = from_blocks(lax.map(blk, (to_blocks(q, 2), jnp.arange(NB))), 2)
    return (rmsnorm(o, subln_g).astype(jnp.float32) * (1.0 - lambda_init)).astype(v.dtype)


def compress_tokens(kv, pos_emb, w1, w2):
    B, G, S, d = kv.shape
    NC = (S - CMP_LEN) // CMP_STRIDE + 1
    gidx = jnp.arange(NC)[:, None] * CMP_STRIDE + jnp.arange(CMP_LEN)[None, :]
    blocks = kv[:, :, gidx] + pos_emb
    hid = jax.nn.silu(blocks.reshape(B, G, NC, CMP_LEN * d) @ w1)
    return hid @ w2


def nsa_attention(q, kc, vc, ks, vs, kw, vw, gates, t5_nsa):
    B, G, R, S, d = q.shape
    NB = S // QBLOCK
    NC = kc.shape[2]
    NS = S // SLC_LEN
    n_sel = min(SLC_TOPN, NS)
    scale = d ** -0.5
    tbl = t5_nsa.astype(jnp.float32).reshape(T5_BUCKETS, G, R)
    tbl_g = jnp.transpose(tbl, (1, 0, 2))
    cmp_end = jnp.arange(NC) * CMP_STRIDE + CMP_LEN - 1
    blk_start = jnp.arange(NS) * SLC_LEN
    overlap = ((cmp_end[:, None] - CMP_LEN + 1 <= blk_start[None, :] + SLC_LEN - 1) & (cmp_end[:, None] >= blk_start[None, :])).astype(jnp.float32)
    ks_b = ks.reshape(B, G, NS, SLC_LEN, d)
    vs_b = vs.reshape(B, G, NS, SLC_LEN, d)
    kw_pad = jnp.pad(kw, ((0, 0), (0, 0), (WINDOW, 0), (0, 0)))
    vw_pad = jnp.pad(vw, ((0, 0), (0, 0), (WINDOW, 0), (0, 0)))
    b_ix = jnp.arange(B)[:, None, None, None]
    g_ix = jnp.arange(G)[None, :, None, None]
    in_blk = jnp.arange(SLC_LEN)
    win_off = jnp.arange(QBLOCK + WINDOW) - WINDOW
    sel_ids = jnp.arange(NS)[None, :]

    def head_bias(bucket):
        return jnp.transpose(tbl[bucket], (2, 3, 0, 1))

    def blk(args):
        qb, gb, i = args
        q0 = i * QBLOCK
        t = q0 + jnp.arange(QBLOCK)
        lc = jnp.einsum('bgrqd,bgnd->bgrqn', qb, kc).astype(jnp.float32) * scale + head_bias(t5_bucket(t[:, None] - cmp_end[None, :]))
        p_cmp = masked_softmax(lc, cmp_end[None, :] <= t[:, None])
        o_cmp = jnp.einsum('bgrqn,bgnd->bgrqd', p_cmp.astype(vc.dtype), vc)
        imp = jnp.einsum('bgrqn,ns->bgqs', p_cmp, overlap)
        cur = (t // SLC_LEN)[:, None]
        forced = (sel_ids == 0) | (sel_ids == cur) | (sel_ids == cur - 1)
        valid = blk_start[None, :] <= t[:, None]
        score = jnp.where(forced, SEL_BIG, jnp.where(valid, imp, -SEL_BIG))
        _, idx = lax.top_k(score, n_sel)
        k_sel = ks_b[b_ix, g_ix, idx].reshape(B, G, QBLOCK, n_sel * SLC_LEN, d)
        v_sel = vs_b[b_ix, g_ix, idx].reshape(B, G, QBLOCK, n_sel * SLC_LEN, d)
        pos = (idx[..., None] * SLC_LEN + in_blk).reshape(B, G, QBLOCK, n_sel * SLC_LEN)
        dist = t[None, None, :, None] - pos
        bias_s = jnp.moveaxis(tbl_g[g_ix, t5_bucket(dist)], -1, 2)
        ls = jnp.einsum('bgrqd,bgqld->bgrql', qb, k_sel).astype(jnp.float32) * scale + bias_s
        p_slc = masked_softmax(ls, (dist >= 0)[:, :, None])
        o_slc = jnp.einsum('bgrql,bgqld->bgrqd', p_slc.astype(v_sel.dtype), v_sel)
        kwin = lax.dynamic_slice_in_dim(kw_pad, q0, QBLOCK + WINDOW, axis=2)
        vwin = lax.dynamic_slice_in_dim(vw_pad, q0, QBLOCK + WINDOW, axis=2)
        s = q0 + win_off
        dw = t[:, None] - s[None, :]
        mw = (dw >= 0) & (dw < WINDOW) & (s[None, :] >= 0)
        lw = jnp.einsum('bgrqd,bgkd->bgrqk', qb, kwin).astype(jnp.float32) * scale + head_bias(t5_bucket(dw))
        p_win = masked_softmax(lw, mw)
        o_win = jnp.einsum('bgrqk,bgkd->bgrqd', p_win.astype(vwin.dtype), vwin)
        g = jnp.transpose(gb.reshape(B, QBLOCK, 3, G, R), (2, 0, 3, 4, 1))[..., None]
        return g[0] * o_cmp + g[1] * o_slc + g[2] * o_win

    out = lax.map(blk, (to_blocks(q, 3), to_blocks(gates, 1), jnp.arange(NB)))
    return from_blocks(out, 3)


def even_mixer(h, w_in, ret_gn, fox_fb, w_out):
    z = h @ w_in
    rq, rk, rv, rg, fq, fk, fv, fl = split_cols(z, EVEN_SPLITS)
    f32 = jnp.float32
    y = retention(heads(rq, RET_HEADS).astype(f32), heads(rk, RET_HEADS).astype(f32), heads(rv, RET_HEADS).astype(f32))
    mu = jnp.mean(y, -1, keepdims=True)
    var = jnp.mean(jnp.square(y - mu), -1, keepdims=True)
    y = merge((y - mu) * lax.rsqrt(var + NORM_EPS)) * ret_gn.astype(f32)
    ret_out = (y * jax.nn.silu(rg.astype(f32))).astype(h.dtype)
    log_f = jax.nn.log_sigmoid(fl.astype(f32) + fox_fb.astype(f32)).transpose(0, 2, 1)
    fox_out = merge(forgetting_attention(heads(fq, FOX_HEADS), heads(fk, FOX_HEADS), heads(fv, FOX_HEADS), log_f))
    return jnp.concatenate([ret_out, fox_out.astype(h.dtype)], -1) @ w_out


def odd_mixer(h, w_in, diff_lambda, diff_subln, cmp_pos, cmp_w1, cmp_w2, w_out, t5_table, lambda_init):
    B, S, _ = h.shape
    z = h @ w_in
    dq, dk, dv, nq, ck, cv, sk, sv, wk, wv, gl = split_cols(z, ODD_SPLITS)
    dq = dq.reshape(B, S, DIFF_HEADS, 2, DIFF_SUB).transpose(0, 2, 1, 3, 4)
    dk = dk.reshape(B, S, DIFF_HEADS, 2, DIFF_SUB).transpose(0, 2, 1, 3, 4)
    diff_out = merge(diff_attention(dq, dk, heads(dv, DIFF_HEADS), diff_lambda, diff_subln, lambda_init, t5_table[:, :DIFF_HEADS]))
    nq = nq.reshape(B, S, NSA_KV_HEADS, NSA_GROUP, HEAD_DIM).transpose(0, 2, 3, 1, 4)
    kc = compress_tokens(heads(ck, NSA_KV_HEADS), cmp_pos[0], cmp_w1[0], cmp_w2[0])
    vc = compress_tokens(heads(cv, NSA_KV_HEADS), cmp_pos[1], cmp_w1[1], cmp_w2[1])
    gates = jax.nn.sigmoid(gl).reshape(B, S, 3, NSA_HEADS)
    o = nsa_attention(nq, kc, vc, heads(sk, NSA_KV_HEADS), heads(sv, NSA_KV_HEADS), heads(wk, NSA_KV_HEADS), heads(wv, NSA_KV_HEADS), gates, t5_table[:, DIFF_HEADS:])
    nsa_out = o.transpose(0, 3, 1, 2, 4).reshape(B, S, NSA_W)
    return jnp.concatenate([diff_out.astype(h.dtype), nsa_out.astype(h.dtype)], -1) @ w_out


def setup_inputs(seed: int = 0) -> dict:
    key = jax.random.key(seed)
    ks = jax.random.split(key, 24)
    f32 = jnp.float32

    def nrm(k, shape, scale):
        return jax.random.normal(k, shape, f32) * scale

    return {
        'x': nrm(ks[0], (BATCH, SEQ, D_MODEL), 1.0),
        'p': nrm(ks[1], (DEPTH, BATCH, SEQ, PLE_DIM), 1.0),
        'norm_mix': 1.0 + nrm(ks[2], (DEPTH, D_MODEL), 0.02),
        'norm_ffn': 1.0 + nrm(ks[3], (DEPTH, D_MODEL), 0.02),
        'w_in_even': nrm(ks[4], (N_EVEN, D_MODEL, EVEN_COLS), D_MODEL ** -0.5),
        'ret_gn': 1.0 + nrm(ks[5], (N_EVEN, RET_W), 0.02),
        'fox_fb': FOX_GATE_BIAS + nrm(ks[6], (N_EVEN, FOX_HEADS), 0.5),
        'w_out_even': nrm(ks[7], (N_EVEN, MIX_W, D_MODEL), MIX_W ** -0.5),
        'w_in_odd': nrm(ks[8], (N_ODD, D_MODEL, ODD_COLS), D_MODEL ** -0.5),
        'diff_lambda': nrm(ks[9], (N_ODD, 4, DIFF_SUB), 0.1),
        'diff_subln': 1.0 + nrm(ks[10], (N_ODD, HEAD_DIM), 0.02),
        'cmp_pos': nrm(ks[11], (N_ODD, 2, CMP_LEN, HEAD_DIM), 0.02),
        'cmp_w1': nrm(ks[12], (N_ODD, 2, CMP_LEN * HEAD_DIM, HEAD_DIM), (CMP_LEN * HEAD_DIM) ** -0.5),
        'cmp_w2': nrm(ks[13], (N_ODD, 2, HEAD_DIM, HEAD_DIM), HEAD_DIM ** -0.5),
        'w_out_odd': nrm(ks[14], (N_ODD, MIX_W, D_MODEL), MIX_W ** -0.5),
        't5_table': nrm(ks[15], (T5_BUCKETS, T5_HEADS), 0.2),
        'ffn_gate': nrm(ks[16], (DEPTH, D_MODEL, D_FF), D_MODEL ** -0.5),
        'ffn_up': nrm(ks[17], (DEPTH, D_MODEL, D_FF), D_MODEL ** -0.5),
        'ffn_down': nrm(ks[18], (DEPTH, D_FF, D_MODEL), D_FF ** -0.5),
        'ple_gate': nrm(ks[19], (DEPTH, D_MODEL, D_MODEL), D_MODEL ** -0.5),
        'ple_proj': nrm(ks[20], (DEPTH, PLE_DIM, D_MODEL), PLE_DIM ** -0.5),
        'final_norm': 1.0 + nrm(ks[21], (D_MODEL,), 0.02),
    }


def reference(x, p, norm_mix, norm_ffn, w_in_even, ret_gn, fox_fb, w_out_even, w_in_odd, diff_lambda, diff_subln, cmp_pos, cmp_w1, cmp_w2, w_out_odd, t5_table, ffn_gate, ffn_up, ffn_down, ple_gate, ple_proj, final_norm):
    for i in range(DEPTH):
        j = i // 2
        h = rmsnorm(x, norm_mix[i])
        if i % 2 == 0:
            mix = even_mixer(h, w_in_even[j], ret_gn[j], fox_fb[j], w_out_even[j])
        else:
            lambda_init = 0.8 - 0.6 * math.exp(-0.3 * i)
            mix = odd_mixer(h, w_in_odd[j], diff_lambda[j], diff_subln[j], cmp_pos[j], cmp_w1[j], cmp_w2[j], w_out_odd[j], t5_table, lambda_init)
        x = x + mix
        h = rmsnorm(x, norm_ffn[i])
        x = x + (jax.nn.silu(h @ ffn_gate[i]) * (h @ ffn_up[i])) @ ffn_down[i]
        x = x + jax.nn.sigmoid(x @ ple_gate[i]) * (p[i] @ ple_proj[i])
    return rmsnorm(x, final_norm)
```

```python
import functools
import math

import numpy as np
import jax
import jax.numpy as jnp
from jax import lax
from jax.experimental import pallas as pl
from jax.experimental.pallas import tpu as pltpu

F32 = jnp.float32
BF16 = jnp.bfloat16

HEAD_DIM = 128
NORM_EPS = 1e-6
RET_HEADS = 8
FOX_HEADS = 8
DIFF_HEADS = 8
NSA_HEADS = 8
NSA_KV_HEADS = 2
NSA_GROUP = NSA_HEADS // NSA_KV_HEADS
RET_CHUNK = 128
CMP_LEN = 32
CMP_STRIDE = 16
SLC_LEN = 64
SLC_TOPN = 16
WINDOW = 512
SEL_BIG = 1e9
T5_BUCKETS = 32
T5_MAX_DIST = 128
PLE_DIM = 256
CMP_QBLOCK = 128
CMP_BAND = 16
CMP_BAND_BACK = 9

_T5_EXACT = T5_BUCKETS // 2
T5_THRESH = tuple(
    b if b <= _T5_EXACT else int(math.ceil(
        _T5_EXACT * (T5_MAX_DIST / _T5_EXACT) ** ((b - _T5_EXACT) / (T5_BUCKETS - _T5_EXACT))))
    for b in range(T5_BUCKETS))

VMEM_LIMIT = 56 * 1024 * 1024
NEG_INF = float("-inf")


def _params(n_axes):
    return pltpu.CompilerParams(dimension_semantics=("arbitrary",) * n_axes,
                                vmem_limit_bytes=VMEM_LIMIT)


def _dot(a, b):
    return jnp.dot(a, b, preferred_element_type=F32)


def _dot_nt(a, b):
    return lax.dot_general(a, b, (((1,), (1,)), ((), ())), preferred_element_type=F32)


def _dot_exact(a, b):
    return jnp.dot(a, b, preferred_element_type=F32, precision=lax.Precision.HIGHEST)


def _split_bf16(x):
    hi = x.astype(BF16)
    r1 = x - hi.astype(F32)
    mid = r1.astype(BF16)
    lo = (r1 - mid.astype(F32)).astype(BF16)
    return hi, mid, lo


def _rmsnorm_kernel(x_ref, g_ref, o_ref):
    x = x_ref[...]
    y = x * lax.rsqrt(jnp.mean(x * x, -1, keepdims=True) + NORM_EPS) * g_ref[...]
    o_ref[...] = y.astype(o_ref.dtype)


def rmsnorm_call(x, g, out_dtype, tm=512):
    m, d = x.shape
    tm = min(tm, m)
    return pl.pallas_call(
        _rmsnorm_kernel,
        grid=(m // tm,),
        in_specs=[pl.BlockSpec((tm, d), lambda i: (i, 0)),
                  pl.BlockSpec((1, d), lambda i: (0, 0))],
        out_specs=pl.BlockSpec((tm, d), lambda i: (i, 0)),
        out_shape=jax.ShapeDtypeStruct((m, d), out_dtype),
        compiler_params=_params(1),
        name="rmsnorm",
    )(x, g.reshape(1, d).astype(F32))


def _mm_plain_kernel(a_ref, w_ref, o_ref):
    o_ref[...] = _dot(a_ref[...], w_ref[...]).astype(o_ref.dtype)


def matmul_call(a, w, out_dtype, tm=1024, tn=512):
    m, k = a.shape
    n = w.shape[1]
    tm, tn = min(tm, m), min(tn, n)
    return pl.pallas_call(
        _mm_plain_kernel,
        grid=(m // tm, n // tn),
        in_specs=[pl.BlockSpec((tm, k), lambda i, j: (i, 0)),
                  pl.BlockSpec((k, tn), lambda i, j: (0, j))],
        out_specs=pl.BlockSpec((tm, tn), lambda i, j: (i, j)),
        out_shape=jax.ShapeDtypeStruct((m, n), out_dtype),
        compiler_params=_params(2),
        name="matmul",
    )(a, w)


def _mm_swiglu_kernel(a_ref, wg_ref, wu_ref, o_ref):
    a = a_ref[...]
    g = _dot(a, wg_ref[...])
    u = _dot(a, wu_ref[...])
    o_ref[...] = (g * jax.nn.sigmoid(g) * u).astype(o_ref.dtype)


def swiglu_call(a, wg, wu, tm=1024, tn=512):
    m, k = a.shape
    n = wg.shape[1]
    tm, tn = min(tm, m), min(tn, n)
    return pl.pallas_call(
        _mm_swiglu_kernel,
        grid=(m // tm, n // tn),
        in_specs=[pl.BlockSpec((tm, k), lambda i, j: (i, 0)),
                  pl.BlockSpec((k, tn), lambda i, j: (0, j)),
                  pl.BlockSpec((k, tn), lambda i, j: (0, j))],
        out_specs=pl.BlockSpec((tm, tn), lambda i, j: (i, j)),
        out_shape=jax.ShapeDtypeStruct((m, n), BF16),
        compiler_params=_params(2),
        name="swiglu",
    )(a, wg, wu)


def _mm_residual_kernel(*refs, n_a, n_b, with_bf16):
    res_ref = refs[0]
    a_refs = refs[1:1 + n_a]
    w1_ref = refs[1 + n_a]
    pos = 2 + n_a
    acc = res_ref[...]

    def summed(group):
        if len(group) == 1:
            return group[0][...]
        tot = group[0][...].astype(F32)
        for r in group[1:]:
            tot = tot + r[...].astype(F32)
        return tot.astype(BF16)

    acc = acc + _dot(summed(a_refs), w1_ref[...])
    if n_b:
        b_refs = refs[pos:pos + n_b]
        w2_ref = refs[pos + n_b]
        pos += n_b + 1
        acc = acc + _dot(summed(b_refs), w2_ref[...])
    refs[pos][...] = acc
    if with_bf16:
        refs[pos + 1][...] = acc.astype(BF16)


def residual_matmul_call(res, a_list, w1, b_list=(), w2=None, with_bf16=False, tm=1024, tn=512):
    m, n = res.shape
    tm, tn = min(tm, m), min(tn, n)
    ka = a_list[0].shape[1]
    args = [res] + list(a_list) + [w1]
    in_specs = [pl.BlockSpec((tm, tn), lambda i, j: (i, j))]
    in_specs += [pl.BlockSpec((tm, ka), lambda i, j: (i, 0)) for _ in a_list]
    in_specs += [pl.BlockSpec((ka, tn), lambda i, j: (0, j))]
    if b_list:
        kb = b_list[0].shape[1]
        args += list(b_list) + [w2]
        in_specs += [pl.BlockSpec((tm, kb), lambda i, j: (i, 0)) for _ in b_list]
        in_specs += [pl.BlockSpec((kb, tn), lambda i, j: (0, j))]
    out_shape = [jax.ShapeDtypeStruct((m, n), F32)]
    out_specs = [pl.BlockSpec((tm, tn), lambda i, j: (i, j))]
    if with_bf16:
        out_shape.append(jax.ShapeDtypeStruct((m, n), BF16))
        out_specs.append(pl.BlockSpec((tm, tn), lambda i, j: (i, j)))
    out = pl.pallas_call(
        functools.partial(_mm_residual_kernel, n_a=len(a_list), n_b=len(b_list), with_bf16=with_bf16),
        grid=(m // tm, n // tn),
        in_specs=in_specs,
        out_specs=out_specs,
        out_shape=out_shape,
        compiler_params=_params(2),
        name="residual_matmul",
    )(*args)
    return out if with_bf16 else out[0]


def _ple_kernel(xb_ref, xres_ref, wg_ref, p_ref, wp_ref, o_ref):
    gate = jax.nn.sigmoid(_dot(xb_ref[...], wg_ref[...]))
    emb = _dot(p_ref[...].astype(BF16), wp_ref[...])
    o_ref[...] = xres_ref[...] + gate * emb


def ple_call(x, xb, wg, p, wp, tm=1024, tn=512):
    m, n = x.shape
    tm, tn = min(tm, m), min(tn, n)
    k = xb.shape[1]
    kp = p.shape[1]
    return pl.pallas_call(
        _ple_kernel,
        grid=(m // tm, n // tn),
        in_specs=[pl.BlockSpec((tm, k), lambda i, j: (i, 0)),
                  pl.BlockSpec((tm, tn), lambda i, j: (i, j)),
                  pl.BlockSpec((k, tn), lambda i, j: (0, j)),
                  pl.BlockSpec((tm, kp), lambda i, j: (i, 0)),
                  pl.BlockSpec((kp, tn), lambda i, j: (0, j))],
        out_specs=pl.BlockSpec((tm, tn), lambda i, j: (i, j)),
        out_shape=jax.ShapeDtypeStruct((m, n), F32),
        compiler_params=_params(2),
        name="ple",
    )(xb, x, wg, p, wp)


def _retention_kernel(q_ref, k_ref, v_ref, g_ref, dec_ref, kw_ref, qw_ref, cd_ref, gn_ref,
                      o_ref, state_ref):
    @pl.when(pl.program_id(1) == 0)
    def _():
        state_ref[...] = jnp.zeros_like(state_ref)

    q = q_ref[...]
    kf = k_ref[...].astype(F32) * (HEAD_DIM ** -0.5)
    v = v_ref[...]
    att = _dot_nt(q, kf.astype(BF16)) * dec_ref[0]
    y = _dot(att.astype(BF16), v)
    state = state_ref[...]
    q_scaled = (q.astype(F32) * qw_ref[0]).astype(BF16)
    y = y + _dot(q_scaled, state.astype(BF16))
    k_scaled_t = (kf * kw_ref[0]).T.astype(BF16)
    state_ref[...] = state * cd_ref[0][0:1, :] + _dot(k_scaled_t, v)
    mu = jnp.mean(y, -1, keepdims=True)
    yc = y - mu
    var = jnp.mean(yc * yc, -1, keepdims=True)
    yn = yc * lax.rsqrt(var + NORM_EPS) * gn_ref[...]
    g = g_ref[...].astype(F32)
    o_ref[...] = (yn * (g * jax.nn.sigmoid(g))).astype(o_ref.dtype)


def retention_call(zb, ret_gn):
    s = zb.shape[0]
    c = RET_CHUNK
    h = RET_HEADS
    log_gamma = jnp.log1p(-jnp.exp2(-5.0 - jnp.arange(h, dtype=F32)))
    pos = jnp.arange(c, dtype=F32)
    rel = pos[:, None] - pos[None, :]
    intra = jnp.where(rel >= 0, jnp.exp(log_gamma[:, None, None] * jnp.maximum(rel, 0.0)), 0.0)
    k_w = jnp.exp(log_gamma[:, None] * (c - 1 - pos))
    q_w = jnp.exp(log_gamma[:, None] * (pos + 1.0))
    chunk_decay = jnp.exp(log_gamma * c)
    kw_b = jnp.broadcast_to(k_w[:, :, None], (h, c, HEAD_DIM))
    qw_b = jnp.broadcast_to(q_w[:, :, None], (h, c, HEAD_DIM))
    cd_b = jnp.broadcast_to(chunk_decay[:, None, None], (h, 8, HEAD_DIM))

    def col(off):
        return pl.BlockSpec((c, HEAD_DIM), lambda hh, n: (n, off + hh))

    def per_head(shape):
        return pl.BlockSpec((1,) + shape, lambda hh, n: (hh, 0, 0))

    return pl.pallas_call(
        _retention_kernel,
        grid=(h, s // c),
        in_specs=[col(0), col(h), col(2 * h), col(3 * h),
                  per_head((c, c)), per_head((c, HEAD_DIM)), per_head((c, HEAD_DIM)),
                  per_head((8, HEAD_DIM)),
                  pl.BlockSpec((1, HEAD_DIM), lambda hh, n: (0, hh))],
        out_specs=pl.BlockSpec((c, HEAD_DIM), lambda hh, n: (n, hh)),
        out_shape=jax.ShapeDtypeStruct((s, h * HEAD_DIM), BF16),
        scratch_shapes=[pltpu.VMEM((HEAD_DIM, HEAD_DIM), F32)],
        compiler_params=_params(2),
        name="retention",
    )(zb, zb, zb, zb, intra, kw_b, qw_b, cd_b, ret_gn.reshape(1, -1).astype(F32))


def _forget_cumsum_kernel(fb_ref, fl_ref, o_ref):
    x = fl_ref[0] + fb_ref[pl.program_id(0)]
    logf = jnp.minimum(x, 0.0) - jnp.log1p(jnp.exp(-jnp.abs(x)))
    rows = x.shape[0]
    upper = (lax.broadcasted_iota(jnp.int32, (128, 128), 0)
             <= lax.broadcasted_iota(jnp.int32, (128, 128), 1)).astype(F32)
    within = _dot_exact(logf, upper)
    totals = jnp.broadcast_to(within[:, 127:128], (rows, 128))
    strict_lower = (lax.broadcasted_iota(jnp.int32, (rows, rows), 1)
                    < lax.broadcasted_iota(jnp.int32, (rows, rows), 0)).astype(F32)
    o_ref[0] = within + _dot_exact(strict_lower, totals)


def forget_cumsum_call(fl_t, fox_fb):
    h, s = fl_t.shape
    rows = s // 128
    out = pl.pallas_call(
        _forget_cumsum_kernel,
        grid=(h,),
        in_specs=[pl.BlockSpec(memory_space=pltpu.SMEM),
                  pl.BlockSpec((1, rows, 128), lambda i: (i, 0, 0))],
        out_specs=pl.BlockSpec((1, rows, 128), lambda i: (i, 0, 0)),
        out_shape=jax.ShapeDtypeStruct((h, rows, 128), F32),
        compiler_params=_params(1),
        name="forget_cumsum",
    )(fox_fb.astype(F32), fl_t.reshape(h, rows, 128))
    return out.reshape(h, s)


def _online_softmax_step(s, v, m_ref, l_ref, acc_ref, idx):
    m_old = m_ref[idx]
    m_new = jnp.maximum(m_old, jnp.max(s, -1, keepdims=True))
    m_safe = jnp.where(m_new == NEG_INF, 0.0, m_new)
    alpha = jnp.exp(m_old - m_safe)
    p = jnp.exp(s - m_safe)
    l_ref[idx] = alpha * l_ref[idx] + jnp.sum(p, -1, keepdims=True)
    acc_ref[idx] = alpha * acc_ref[idx] + _dot(p.astype(BF16), v)
    m_ref[idx] = m_new


def _init_softmax_state(m_ref, l_ref, acc_ref):
    m_ref[...] = jnp.full(m_ref.shape, NEG_INF, F32)
    l_ref[...] = jnp.zeros(l_ref.shape, F32)
    acc_ref[...] = jnp.zeros(acc_ref.shape, F32)


def _tile_rows_cols(t):
    return (lax.broadcasted_iota(jnp.int32, (t, t), 0),
            lax.broadcasted_iota(jnp.int32, (t, t), 1))


def _fox_kernel(q_ref, k_ref, v_ref, ccol_ref, crow_ref, o_ref, m_ref, l_ref, acc_ref, *, t):
    qi = pl.program_id(1)
    q = q_ref[...]
    cq = ccol_ref[0]
    _init_softmax_state(m_ref, l_ref, acc_ref)
    scale = HEAD_DIM ** -0.5

    def tile(ki, causal):
        start = pl.multiple_of(ki * t, t)
        k = k_ref[pl.ds(start, t), :]
        v = v_ref[pl.ds(start, t), :]
        s = _dot_nt(q, k) * scale + (cq - crow_ref[0, ki])
        if causal:
            rows, cols = _tile_rows_cols(t)
            s = jnp.where(cols <= rows, s, NEG_INF)
        _online_softmax_step(s, v, m_ref, l_ref, acc_ref, 0)

    def body(ki, carry):
        tile(ki, False)
        return carry

    lax.fori_loop(0, qi, body, 0)
    tile(qi, True)
    o_ref[...] = (acc_ref[0] / l_ref[0]).astype(o_ref.dtype)


def fox_attention_call(zb, c, q_off, k_off, v_off, t):
    s = zb.shape[0]
    h = FOX_HEADS
    nt = s // t
    c_col = c.reshape(h, s, 1)
    c_row = c.reshape(h, nt, 1, t)
    return pl.pallas_call(
        functools.partial(_fox_kernel, t=t),
        grid=(h, nt),
        in_specs=[pl.BlockSpec((t, HEAD_DIM), lambda hh, i: (i, q_off + hh)),
                  pl.BlockSpec((s, HEAD_DIM), lambda hh, i: (0, k_off + hh)),
                  pl.BlockSpec((s, HEAD_DIM), lambda hh, i: (0, v_off + hh)),
                  pl.BlockSpec((1, t, 1), lambda hh, i: (hh, i, 0)),
                  pl.BlockSpec((1, nt, 1, t), lambda hh, i: (hh, 0, 0, 0))],
        out_specs=pl.BlockSpec((t, HEAD_DIM), lambda hh, i: (i, hh)),
        out_shape=jax.ShapeDtypeStruct((s, h * HEAD_DIM), BF16),
        scratch_shapes=[pltpu.VMEM((1, t, 1), F32), pltpu.VMEM((1, t, 1), F32),
                        pltpu.VMEM((1, t, HEAD_DIM), F32)],
        compiler_params=_params(2),
        name="fox_attention",
    )(zb, zb, zb, c_col, c_row)


def _t5_bias(tbl_ref, head, dist):
    bias = jnp.full(dist.shape, tbl_ref[0, head], F32)
    for b in range(1, T5_BUCKETS):
        bias = jnp.where(dist >= T5_THRESH[b], tbl_ref[b, head], bias)
    return bias


def _fill_t5_tiles(tbl_ref, head, bias_ref, slot, t):
    rows, cols = _tile_rows_cols(t)
    dist = rows - cols
    bias_ref[slot, 0] = _t5_bias(tbl_ref, head, dist)
    bias_ref[slot, 1] = _t5_bias(tbl_ref, head, dist + t)


def _diff_kernel(tbl_ref, q_ref, k_ref, v_ref, lam_ref, g_ref, o_ref,
                 m_ref, l_ref, acc_ref, bias_ref, *, t, lambda_init):
    head = pl.program_id(0)
    qi = pl.program_id(1)

    @pl.when(qi == 0)
    def _():
        _fill_t5_tiles(tbl_ref, head, bias_ref, 0, t)

    q = q_ref[...]
    lane = lax.broadcasted_iota(jnp.int32, q.shape, 1)
    zero = jnp.zeros_like(q)
    q_maps = (jnp.where(lane < HEAD_DIM // 2, q, zero), jnp.where(lane >= HEAD_DIM // 2, q, zero))
    _init_softmax_state(m_ref, l_ref, acc_ref)
    scale = (HEAD_DIM // 2) ** -0.5
    far_bias = tbl_ref[T5_BUCKETS - 1, head]

    def tile(ki, kind):
        start = pl.multiple_of(ki * t, t)
        k = k_ref[pl.ds(start, t), :]
        v = v_ref[pl.ds(start, t), :]
        if kind == "far":
            bias = far_bias
        elif kind == "near":
            bias = bias_ref[0, 1]
        else:
            bias = bias_ref[0, 0]
        for mi in range(2):
            s = _dot_nt(q_maps[mi], k) * scale + bias
            if kind == "diag":
                rows, cols = _tile_rows_cols(t)
                s = jnp.where(cols <= rows, s, NEG_INF)
            _online_softmax_step(s, v, m_ref, l_ref, acc_ref, mi)

    def body(ki, carry):
        tile(ki, "far")
        return carry

    lax.fori_loop(0, jnp.maximum(qi - 1, 0), body, 0)

    @pl.when(qi >= 1)
    def _():
        tile(qi - 1, "near")

    tile(qi, "diag")

    lam = lam_ref[...]
    lmbda = (jnp.exp(jnp.sum(lam[0:1] * lam[1:2], keepdims=True))
             - jnp.exp(jnp.sum(lam[2:3] * lam[3:4], keepdims=True)) + lambda_init)
    o = acc_ref[0] / l_ref[0] - lmbda * (acc_ref[1] / l_ref[1])
    y = o * lax.rsqrt(jnp.mean(o * o, -1, keepdims=True) + NORM_EPS) * g_ref[...]
    o_ref[...] = (y * (1.0 - lambda_init)).astype(o_ref.dtype)


def diff_attention_call(zb, t5_table, diff_lambda, subln, lambda_init, q_off, k_off, v_off, t):
    s = zb.shape[0]
    h = DIFF_HEADS
    nt = s // t
    return pl.pallas_call(
        functools.partial(_diff_kernel, t=t, lambda_init=lambda_init),
        grid=(h, nt),
        in_specs=[pl.BlockSpec(memory_space=pltpu.SMEM),
                  pl.BlockSpec((t, HEAD_DIM), lambda hh, i: (i, q_off + hh)),
                  pl.BlockSpec((s, HEAD_DIM), lambda hh, i: (0, k_off + hh)),
                  pl.BlockSpec((s, HEAD_DIM), lambda hh, i: (0, v_off + hh)),
                  pl.BlockSpec(diff_lambda.shape, lambda hh, i: (0, 0)),
                  pl.BlockSpec((1, HEAD_DIM), lambda hh, i: (0, 0))],
        out_specs=pl.BlockSpec((t, HEAD_DIM), lambda hh, i: (i, hh)),
        out_shape=jax.ShapeDtypeStruct((s, h * HEAD_DIM), BF16),
        scratch_shapes=[pltpu.VMEM((2, t, 1), F32), pltpu.VMEM((2, t, 1), F32),
                        pltpu.VMEM((2, t, HEAD_DIM), F32), pltpu.VMEM((1, 2, t, t), F32)],
        compiler_params=_params(2),
        name="diff_attention",
    )(t5_table.astype(F32), zb, zb, zb, diff_lambda.astype(F32), subln.reshape(1, -1).astype(F32))


def _compress_kernel(kr_ref, pos_ref, w1_ref, w2_ref, o_ref):
    half = w1_ref.shape[1] // 2
    kr = kr_ref[0, 0]
    w_top = w1_ref[0, :half, :]
    w_bot = w1_ref[0, half:, :]
    pos = pos_ref[0].astype(BF16)
    first = _dot(kr, w_top)
    second = _dot(kr, w_bot)
    pos_part = _dot(pos[:, :half], w_top) + _dot(pos[:, half:], w_bot)
    rows = kr.shape[0]
    second_next = pltpu.roll(second, rows - 1, 0)
    hid = first + second_next + pos_part[0:1, :]
    hid = hid * jax.nn.sigmoid(hid)
    o_ref[0, 0] = _dot(hid.astype(BF16), w2_ref[0]).astype(o_ref.dtype)


def compress_call(kv_rows, pos, w1, w2):
    two, g, nc, width = kv_rows.shape
    return pl.pallas_call(
        _compress_kernel,
        grid=(two, g),
        in_specs=[pl.BlockSpec((1, 1, nc, width), lambda a, b: (a, b, 0, 0)),
                  pl.BlockSpec((1, 16, 2 * width), lambda a, b: (a, 0, 0)),
                  pl.BlockSpec((1, 2 * width, HEAD_DIM), lambda a, b: (a, 0, 0)),
                  pl.BlockSpec((1, HEAD_DIM, HEAD_DIM), lambda a, b: (a, 0, 0))],
        out_specs=pl.BlockSpec((1, 1, nc, HEAD_DIM), lambda a, b: (a, b, 0, 0)),
        out_shape=jax.ShapeDtypeStruct((two, g, nc, HEAD_DIM), BF16),
        compiler_params=_params(2),
        name="nsa_compress",
    )(kv_rows, pos, w1, w2)


def _nsa_cmp_kernel(tbl_ref, q_ref, kc_ref, vc_ref, gate_ref, o_ref, sel_ref, *, head0):
    grp = pl.program_id(0)
    qb = pl.program_id(1)
    tq = q_ref.shape[0]
    ncp = kc_ref.shape[2]
    ns = sel_ref.shape[2]
    kc = kc_ref[0, 0]
    vc = vc_ref[0, 0]
    scale = HEAD_DIM ** -0.5

    t_pos = qb * tq + lax.broadcasted_iota(jnp.int32, (tq, ncp), 0)
    cmp_end = lax.broadcasted_iota(jnp.int32, (tq, ncp), 1) * CMP_STRIDE + (CMP_LEN - 1)
    visible = cmp_end <= t_pos

    band_rows = lax.broadcasted_iota(jnp.int32, (tq, tq), 0)
    band_lane = lax.broadcasted_iota(jnp.int32, (tq, tq), 1)
    band_dist = band_rows - (band_lane - CMP_BAND_BACK) * CMP_STRIDE - (CMP_LEN - 1)
    in_band = band_lane < CMP_BAND
    first_tok = qb * (tq // CMP_STRIDE) - CMP_BAND_BACK
    place = (lax.broadcasted_iota(jnp.int32, (tq, ncp), 1)
             == first_tok + lax.broadcasted_iota(jnp.int32, (tq, ncp), 0))
    place = jnp.where(place, 1.0, 0.0).astype(BF16)

    gates = jax.nn.sigmoid(gate_ref[0, 0])
    p_sum = jnp.zeros((tq, ncp), F32)
    for r in range(NSA_GROUP):
        head = head0 + grp * NSA_GROUP + r
        far = tbl_ref[T5_BUCKETS - 1, head]
        delta = jnp.where(in_band, _t5_bias(tbl_ref, head, band_dist) - far, 0.0)
        d_hi, d_mid, d_lo = _split_bf16(delta)
        bias = far + (_dot(d_hi, place) + _dot(d_mid, place) + _dot(d_lo, place))
        q = q_ref[:, r * HEAD_DIM:(r + 1) * HEAD_DIM]
        logits = jnp.where(visible, _dot_nt(q, kc) * scale + bias, NEG_INF)
        m = jnp.max(logits, -1, keepdims=True)
        m = jnp.where(m == NEG_INF, 0.0, m)
        e = jnp.where(visible, jnp.exp(logits - m), 0.0)
        p = e / jnp.maximum(jnp.sum(e, -1, keepdims=True), 1e-30)
        p_sum = p_sum + p
        o = _dot(p.astype(BF16), vc) * gates[:, r:r + 1]
        o_ref[:, r * HEAD_DIM:(r + 1) * HEAD_DIM] = o.astype(o_ref.dtype)

    ci = lax.broadcasted_iota(jnp.int32, (ncp, ns), 0) * CMP_STRIDE
    bj = lax.broadcasted_iota(jnp.int32, (ncp, ns), 1) * SLC_LEN
    overlap = jnp.where((ci <= bj + SLC_LEN - 1) & (ci + CMP_LEN - 1 >= bj), 1.0, 0.0).astype(BF16)
    p_hi, p_mid, p_lo = _split_bf16(p_sum)
    imp = _dot(p_hi, overlap) + _dot(p_mid, overlap) + _dot(p_lo, overlap)

    t_sel = qb * tq + lax.broadcasted_iota(jnp.int32, (tq, ns), 0)
    blk = lax.broadcasted_iota(jnp.int32, (tq, ns), 1)
    cur = t_sel // SLC_LEN
    forced = (blk == 0) | (blk == cur) | (blk == cur - 1)
    valid = blk * SLC_LEN <= t_sel
    score = jnp.where(forced, SEL_BIG, jnp.where(valid, imp, -SEL_BIG))
    blk_f = blk.astype(F32)

    def pick(_, carry):
        sc, chosen = carry
        best = jnp.max(sc, -1, keepdims=True)
        first = jnp.min(jnp.where(sc == best, blk_f, float(ns)), -1, keepdims=True)
        hit = blk_f == first
        return jnp.where(hit, NEG_INF, sc), jnp.where(hit, 1.0, chosen)

    _, chosen = lax.fori_loop(0, min(SLC_TOPN, ns), pick, (score, jnp.zeros((tq, ns), F32)))
    sel_ref[0] = chosen.astype(sel_ref.dtype)


def nsa_cmp_call(zb, kvc, gates, t5_table, q_off, ns):
    s = zb.shape[0]
    g = NSA_KV_HEADS
    tq = CMP_QBLOCK
    ncp = kvc.shape[2]
    width = NSA_GROUP * HEAD_DIM
    return pl.pallas_call(
        functools.partial(_nsa_cmp_kernel, head0=DIFF_HEADS),
        grid=(g, s // tq),
        in_specs=[pl.BlockSpec(memory_space=pltpu.SMEM),
                  pl.BlockSpec((tq, width), lambda gg, i: (i, q_off + gg)),
                  pl.BlockSpec((1, 1, ncp, HEAD_DIM), lambda gg, i: (0, gg, 0, 0)),
                  pl.BlockSpec((1, 1, ncp, HEAD_DIM), lambda gg, i: (1, gg, 0, 0)),
                  pl.BlockSpec((1, 1, tq, NSA_GROUP), lambda gg, i: (0, gg, i, 0))],
        out_specs=[pl.BlockSpec((tq, width), lambda gg, i: (i, gg)),
                   pl.BlockSpec((1, tq, ns), lambda gg, i: (gg, i, 0))],
        out_shape=[jax.ShapeDtypeStruct((s, g * width), BF16),
                   jax.ShapeDtypeStruct((g, s, ns), BF16)],
        compiler_params=_params(2),
        name="nsa_compressed",
    )(t5_table.astype(F32), zb, kvc, kvc, gates)


def _nsa_sweep_kernel(tbl_ref, q_ref, k_ref, v_ref, gate_ref, *rest, t, mode, head0, branch):
    if mode == "selected":
        sel_ref, o_ref, m_ref, l_ref, acc_ref, bias_ref = rest
    else:
        o_ref, m_ref, l_ref, acc_ref, bias_ref = rest
    grp = pl.program_id(0)
    qi = pl.program_id(1)

    @pl.when(qi == 0)
    def _():
        for r in range(NSA_GROUP):
            _fill_t5_tiles(tbl_ref, head0 + grp * NSA_GROUP + r, bias_ref, r, t)

    _init_softmax_state(m_ref, l_ref, acc_ref)
    scale = HEAD_DIM ** -0.5
    rows, cols = _tile_rows_cols(t)

    def tile(ki, kind):
        start = pl.multiple_of(ki * t, t)
        k = k_ref[pl.ds(start, t), :]
        v = v_ref[pl.ds(start, t), :]
        if mode == "selected":
            ns = sel_ref.shape[2]
            blk_of_key = (ki * t + lax.broadcasted_iota(jnp.int32, (ns, t), 1)) // SLC_LEN
            expand = jnp.where(lax.broadcasted_iota(jnp.int32, (ns, t), 0) == blk_of_key, 1.0, 0.0)
            keep = _dot(sel_ref[0], expand.astype(BF16)) > 0.5
            if kind == "diag":
                keep = keep & (cols <= rows)
        else:
            keep = (cols <= rows) if kind == "diag" else (cols > rows)
        for r in range(NSA_GROUP):
            if kind == "far":
                bias = tbl_ref[T5_BUCKETS - 1, head0 + grp * NSA_GROUP + r]
            elif kind == "near":
                bias = bias_ref[r, 1]
            else:
                bias = bias_ref[r, 0]
            q = q_ref[:, r * HEAD_DIM:(r + 1) * HEAD_DIM]
            s = jnp.where(keep, _dot_nt(q, k) * scale + bias, NEG_INF)
            _online_softmax_step(s, v, m_ref, l_ref, acc_ref, r)

    if mode == "selected":
        def body(ki, carry):
            tile(ki, "far")
            return carry

        lax.fori_loop(0, jnp.maximum(qi - 1, 0), body, 0)

    @pl.when(qi >= 1)
    def _():
        tile(qi - 1, "near")

    tile(qi, "diag")

    gates = jax.nn.sigmoid(gate_ref[0, 0])
    for r in range(NSA_GROUP):
        o = acc_ref[r] / l_ref[r] * gates[:, r:r + 1]
        o_ref[:, r * HEAD_DIM:(r + 1) * HEAD_DIM] = o.astype(o_ref.dtype)


def nsa_sweep_call(zb, gates, t5_table, sel, q_off, k_off, v_off, t, mode):
    s = zb.shape[0]
    g = NSA_KV_HEADS
    nt = s // t
    width = NSA_GROUP * HEAD_DIM
    branch = 1 if mode == "selected" else 2
    if mode == "window":
        assert t == WINDOW
    in_specs = [pl.BlockSpec(memory_space=pltpu.SMEM),
                pl.BlockSpec((t, width), lambda gg, i: (i, q_off + gg)),
                pl.BlockSpec((s, HEAD_DIM), lambda gg, i: (0, k_off + gg)),
                pl.BlockSpec((s, HEAD_DIM), lambda gg, i: (0, v_off + gg)),
                pl.BlockSpec((1, 1, t, NSA_GROUP), lambda gg, i: (branch, gg, i, 0))]
    args = [t5_table.astype(F32), zb, zb, zb, gates]
    if mode == "selected":
        ns = sel.shape[2]
        in_specs.append(pl.BlockSpec((1, t, ns), lambda gg, i: (gg, i, 0)))
        args.append(sel)
    return pl.pallas_call(
        functools.partial(_nsa_sweep_kernel, t=t, mode=mode, head0=DIFF_HEADS, branch=branch),
        grid=(g, nt),
        in_specs=in_specs,
        out_specs=pl.BlockSpec((t, width), lambda gg, i: (i, gg)),
        out_shape=jax.ShapeDtypeStruct((s, g * width), BF16),
        scratch_shapes=[pltpu.VMEM((NSA_GROUP, t, 1), F32), pltpu.VMEM((NSA_GROUP, t, 1), F32),
                        pltpu.VMEM((NSA_GROUP, t, HEAD_DIM), F32),
                        pltpu.VMEM((NSA_GROUP, 2, t, t), F32)],
        compiler_params=_params(2),
        name="nsa_" + mode,
    )(*args)


def _pad_cols(w, n):
    return jnp.pad(w, ((0, 0), (0, n - w.shape[1])))


def even_mixer(x, norm_g, w_in, ret_gn, fox_fb, w_out, t):
    s = x.shape[0]
    hb = rmsnorm_call(x, norm_g, BF16)
    main = 7 * RET_HEADS * HEAD_DIM
    zb = matmul_call(hb, w_in[:, :main].astype(BF16), BF16)
    zg = matmul_call(hb, _pad_cols(w_in[:, main:], HEAD_DIM).astype(BF16), F32, tn=HEAD_DIM)
    ret_out = retention_call(zb, ret_gn)
    c = forget_cumsum_call(zg[:, :FOX_HEADS].T, fox_fb)
    fox_out = fox_attention_call(zb, c, 4 * RET_HEADS, 5 * RET_HEADS, 6 * RET_HEADS, t)
    half = RET_HEADS * HEAD_DIM
    w_out_b = w_out.astype(BF16)
    return residual_matmul_call(x, [ret_out], w_out_b[:half], [fox_out], w_out_b[half:])


def odd_mixer(x, norm_g, w_in, diff_lambda, diff_subln, cmp_pos, cmp_w1, cmp_w2, w_out, t5_table,
              lambda_init, t):
    s = x.shape[0]
    hb = rmsnorm_call(x, norm_g, BF16)
    main = 5632
    zb = matmul_call(hb, w_in[:, :main].astype(BF16), BF16)
    zg = matmul_call(hb, _pad_cols(w_in[:, main:], HEAD_DIM).astype(BF16), F32, tn=HEAD_DIM)
    hd = HEAD_DIM
    diff_out = diff_attention_call(zb, t5_table, diff_lambda, diff_subln, lambda_init, 0, 8, 16, t)
    g = NSA_KV_HEADS
    ckv = zb[:, 32 * hd:36 * hd].reshape(s // CMP_STRIDE, CMP_STRIDE, 2, g, hd)
    kv_rows = ckv.transpose(2, 3, 0, 1, 4).reshape(2, g, s // CMP_STRIDE, CMP_STRIDE * hd)
    pos = jnp.broadcast_to(cmp_pos.reshape(2, 1, CMP_LEN * hd), (2, 16, CMP_LEN * hd)).astype(F32)
    kvc = compress_call(kv_rows, pos, cmp_w1.astype(BF16), cmp_w2.astype(BF16))
    gates = zg[:, :3 * NSA_HEADS].reshape(s, 3, g, NSA_GROUP).transpose(1, 2, 0, 3)
    ns = s // SLC_LEN
    o_cmp, sel = nsa_cmp_call(zb, kvc, gates, t5_table, 6, ns)
    o_slc = nsa_sweep_call(zb, gates, t5_table, sel, 6, 36, 38, t, "selected")
    o_win = nsa_sweep_call(zb, gates, t5_table, None, 6, 40, 42, WINDOW, "window")
    half = DIFF_HEADS * hd
    w_out_b = w_out.astype(BF16)
    return residual_matmul_call(x, [diff_out], w_out_b[:half], [o_cmp, o_slc, o_win], w_out_b[half:])


def ffn_and_ple(x, norm_g, w_gate, w_up, w_down, p, ple_gate, ple_proj):
    hb = rmsnorm_call(x, norm_g, BF16)
    u = swiglu_call(hb, w_gate.astype(BF16), w_up.astype(BF16))
    x2, x2b = residual_matmul_call(x, [u], w_down.astype(BF16), with_bf16=True)
    return ple_call(x2, x2b, ple_gate.astype(BF16), p, ple_proj.astype(BF16))


def _attention_tile(s):
    return min(512, s)


def kernel(x, p, norm_mix, norm_ffn, w_in_even, ret_gn, fox_fb, w_out_even, w_in_odd, diff_lambda, diff_subln, cmp_pos, cmp_w1, cmp_w2, w_out_odd, t5_table, ffn_gate, ffn_up, ffn_down, ple_gate, ple_proj, final_norm):
    batch, s, d = x.shape
    assert batch == 1
    depth = p.shape[0]
    t = _attention_tile(s)
    xs = x[0]
    for i in range(depth):
        j = i // 2
        if i % 2 == 0:
            xs = even_mixer(xs, norm_mix[i], w_in_even[j], ret_gn[j], fox_fb[j], w_out_even[j], t)
        else:
            lambda_init = 0.8 - 0.6 * math.exp(-0.3 * i)
            xs = odd_mixer(xs, norm_mix[i], w_in_odd[j], diff_lambda[j], diff_subln[j], cmp_pos[j],
                           cmp_w1[j], cmp_w2[j], w_out_odd[j], t5_table, lambda_init, t)
        xs = ffn_and_ple(xs, norm_ffn[i], ffn_gate[i], ffn_up[i], ffn_down[i], p[i, 0],
                         ple_gate[i], ple_proj[i])
    out = rmsnorm_call(xs, final_norm, F32)
    return out[None]
```

```python
import functools
import math

import numpy as np
import jax
import jax.numpy as jnp
from jax import lax
from jax.experimental import pallas as pl
from jax.experimental.pallas import tpu as pltpu

F32 = jnp.float32
BF16 = jnp.bfloat16

HEAD_DIM = 128
NORM_EPS = 1e-6
RET_HEADS = 8
FOX_HEADS = 8
DIFF_HEADS = 8
NSA_HEADS = 8
NSA_KV_HEADS = 2
NSA_GROUP = NSA_HEADS // NSA_KV_HEADS
RET_CHUNK = 128
CMP_LEN = 32
CMP_STRIDE = 16
SLC_LEN = 64
SLC_TOPN = 16
WINDOW = 512
SEL_BIG = 1e9
T5_BUCKETS = 32
T5_MAX_DIST = 128
PLE_DIM = 256
CMP_QBLOCK = 128
CMP_BAND = 16
CMP_BAND_BACK = 9

_T5_EXACT = T5_BUCKETS // 2
T5_THRESH = tuple(
    b if b <= _T5_EXACT else int(math.ceil(
        _T5_EXACT * (T5_MAX_DIST / _T5_EXACT) ** ((b - _T5_EXACT) / (T5_BUCKETS - _T5_EXACT))))
    for b in range(T5_BUCKETS))

VMEM_LIMIT = 56 * 1024 * 1024
NEG_INF = float("-inf")
LANES = 128
LOG2E = math.log2(math.e)


def _params(n_axes):
    return pltpu.CompilerParams(dimension_semantics=("arbitrary",) * n_axes,
                                vmem_limit_bytes=VMEM_LIMIT)


def _dot(a, b):
    return jnp.dot(a, b, preferred_element_type=F32)


def _dot_nt(a, b):
    return lax.dot_general(a, b, (((1,), (1,)), ((), ())), preferred_element_type=F32)


def _dot_exact(a, b):
    return jnp.dot(a, b, preferred_element_type=F32, precision=lax.Precision.HIGHEST)


def _split_bf16(x):
    hi = x.astype(BF16)
    r1 = x - hi.astype(F32)
    mid = r1.astype(BF16)
    lo = (r1 - mid.astype(F32)).astype(BF16)
    return hi, mid, lo


def _rmsnorm_kernel(x_ref, g_ref, o_ref):
    x = x_ref[...]
    y = x * lax.rsqrt(jnp.mean(x * x, -1, keepdims=True) + NORM_EPS) * g_ref[...]
    o_ref[...] = y.astype(o_ref.dtype)


def rmsnorm_call(x, g, out_dtype, tm=512):
    m, d = x.shape
    tm = min(tm, m)
    return pl.pallas_call(
        _rmsnorm_kernel,
        grid=(m // tm,),
        in_specs=[pl.BlockSpec((tm, d), lambda i: (i, 0)),
                  pl.BlockSpec((1, d), lambda i: (0, 0))],
        out_specs=pl.BlockSpec((tm, d), lambda i: (i, 0)),
        out_shape=jax.ShapeDtypeStruct((m, d), out_dtype),
        compiler_params=_params(1),
        name="rmsnorm",
    )(x, g.reshape(1, d).astype(F32))


def _mm_plain_kernel(a_ref, w_ref, o_ref):
    o_ref[...] = _dot(a_ref[...], w_ref[...]).astype(o_ref.dtype)


def matmul_call(a, w, out_dtype, tm=1024, tn=512):
    m, k = a.shape
    n = w.shape[1]
    tm, tn = min(tm, m), min(tn, n)
    return pl.pallas_call(
        _mm_plain_kernel,
        grid=(m // tm, n // tn),
        in_specs=[pl.BlockSpec((tm, k), lambda i, j: (i, 0)),
                  pl.BlockSpec((k, tn), lambda i, j: (0, j))],
        out_specs=pl.BlockSpec((tm, tn), lambda i, j: (i, j)),
        out_shape=jax.ShapeDtypeStruct((m, n), out_dtype),
        compiler_params=_params(2),
        name="matmul",
    )(a, w)


def _mm_swiglu_kernel(a_ref, wg_ref, wu_ref, o_ref):
    a = a_ref[...]
    g = _dot(a, wg_ref[...])
    u = _dot(a, wu_ref[...])
    o_ref[...] = (g * jax.nn.sigmoid(g) * u).astype(o_ref.dtype)


def swiglu_call(a, wg, wu, tm=1024, tn=512):
    m, k = a.shape
    n = wg.shape[1]
    tm, tn = min(tm, m), min(tn, n)
    return pl.pallas_call(
        _mm_swiglu_kernel,
        grid=(m // tm, n // tn),
        in_specs=[pl.BlockSpec((tm, k), lambda i, j: (i, 0)),
                  pl.BlockSpec((k, tn), lambda i, j: (0, j)),
                  pl.BlockSpec((k, tn), lambda i, j: (0, j))],
        out_specs=pl.BlockSpec((tm, tn), lambda i, j: (i, j)),
        out_shape=jax.ShapeDtypeStruct((m, n), BF16),
        compiler_params=_params(2),
        name="swiglu",
    )(a, wg, wu)


def _mm_residual_kernel(*refs, n_a, n_b, with_bf16):
    res_ref = refs[0]
    a_refs = refs[1:1 + n_a]
    w1_ref = refs[1 + n_a]
    pos = 2 + n_a
    acc = res_ref[...]

    def summed(group):
        if len(group) == 1:
            return group[0][...]
        tot = group[0][...].astype(F32)
        for r in group[1:]:
            tot = tot + r[...].astype(F32)
        return tot.astype(BF16)

    acc = acc + _dot(summed(a_refs), w1_ref[...])
    if n_b:
        b_refs = refs[pos:pos + n_b]
        w2_ref = refs[pos + n_b]
        pos += n_b + 1
        acc = acc + _dot(summed(b_refs), w2_ref[...])
    refs[pos][...] = acc
    if with_bf16:
        refs[pos + 1][...] = acc.astype(BF16)


def residual_matmul_call(res, a_list, w1, b_list=(), w2=None, with_bf16=False, tm=1024, tn=512):
    m, n = res.shape
    tm, tn = min(tm, m), min(tn, n)
    ka = a_list[0].shape[1]
    args = [res] + list(a_list) + [w1]
    in_specs = [pl.BlockSpec((tm, tn), lambda i, j: (i, j))]
    in_specs += [pl.BlockSpec((tm, ka), lambda i, j: (i, 0)) for _ in a_list]
    in_specs += [pl.BlockSpec((ka, tn), lambda i, j: (0, j))]
    if b_list:
        kb = b_list[0].shape[1]
        args += list(b_list) + [w2]
        in_specs += [pl.BlockSpec((tm, kb), lambda i, j: (i, 0)) for _ in b_list]
        in_specs += [pl.BlockSpec((kb, tn), lambda i, j: (0, j))]
    out_shape = [jax.ShapeDtypeStruct((m, n), F32)]
    out_specs = [pl.BlockSpec((tm, tn), lambda i, j: (i, j))]
    if with_bf16:
        out_shape.append(jax.ShapeDtypeStruct((m, n), BF16))
        out_specs.append(pl.BlockSpec((tm, tn), lambda i, j: (i, j)))
    out = pl.pallas_call(
        functools.partial(_mm_residual_kernel, n_a=len(a_list), n_b=len(b_list), with_bf16=with_bf16),
        grid=(m // tm, n // tn),
        in_specs=in_specs,
        out_specs=out_specs,
        out_shape=out_shape,
        compiler_params=_params(2),
        name="residual_matmul",
    )(*args)
    return out if with_bf16 else out[0]


def _ple_kernel(xb_ref, xres_ref, wg_ref, p_ref, wp_ref, o_ref):
    gate = jax.nn.sigmoid(_dot(xb_ref[...], wg_ref[...]))
    emb = _dot(p_ref[...].astype(BF16), wp_ref[...])
    o_ref[...] = xres_ref[...] + gate * emb


def ple_call(x, xb, wg, p, wp, tm=1024, tn=512):
    m, n = x.shape
    tm, tn = min(tm, m), min(tn, n)
    k = xb.shape[1]
    kp = p.shape[1]
    return pl.pallas_call(
        _ple_kernel,
        grid=(m // tm, n // tn),
        in_specs=[pl.BlockSpec((tm, k), lambda i, j: (i, 0)),
                  pl.BlockSpec((tm, tn), lambda i, j: (i, j)),
                  pl.BlockSpec((k, tn), lambda i, j: (0, j)),
                  pl.BlockSpec((tm, kp), lambda i, j: (i, 0)),
                  pl.BlockSpec((kp, tn), lambda i, j: (0, j))],
        out_specs=pl.BlockSpec((tm, tn), lambda i, j: (i, j)),
        out_shape=jax.ShapeDtypeStruct((m, n), F32),
        compiler_params=_params(2),
        name="ple",
    )(xb, x, wg, p, wp)


def _retention_kernel(q_ref, k_ref, v_ref, g_ref, dec_ref, kw_ref, qw_ref, cd_ref, gn_ref,
                      o_ref, state_ref):
    @pl.when(pl.program_id(1) == 0)
    def _():
        state_ref[...] = jnp.zeros_like(state_ref)

    q = q_ref[...]
    kf = k_ref[...].astype(F32) * (HEAD_DIM ** -0.5)
    v = v_ref[...]
    att = _dot_nt(q, kf.astype(BF16)) * dec_ref[0]
    y = _dot(att.astype(BF16), v)
    state = state_ref[...]
    q_scaled = (q.astype(F32) * qw_ref[0]).astype(BF16)
    y = y + _dot(q_scaled, state.astype(BF16))
    k_scaled_t = (kf * kw_ref[0]).T.astype(BF16)
    state_ref[...] = state * cd_ref[0][0:1, :] + _dot(k_scaled_t, v)
    mu = jnp.mean(y, -1, keepdims=True)
    yc = y - mu
    var = jnp.mean(yc * yc, -1, keepdims=True)
    yn = yc * lax.rsqrt(var + NORM_EPS) * gn_ref[...]
    g = g_ref[...].astype(F32)
    o_ref[...] = (yn * (g * jax.nn.sigmoid(g))).astype(o_ref.dtype)


def retention_call(zb, ret_gn):
    s = zb.shape[0]
    c = RET_CHUNK
    h = RET_HEADS
    log_gamma = jnp.log1p(-jnp.exp2(-5.0 - jnp.arange(h, dtype=F32)))
    pos = jnp.arange(c, dtype=F32)
    rel = pos[:, None] - pos[None, :]
    intra = jnp.where(rel >= 0, jnp.exp(log_gamma[:, None, None] * jnp.maximum(rel, 0.0)), 0.0)
    k_w = jnp.exp(log_gamma[:, None] * (c - 1 - pos))
    q_w = jnp.exp(log_gamma[:, None] * (pos + 1.0))
    chunk_decay = jnp.exp(log_gamma * c)
    kw_b = jnp.broadcast_to(k_w[:, :, None], (h, c, HEAD_DIM))
    qw_b = jnp.broadcast_to(q_w[:, :, None], (h, c, HEAD_DIM))
    cd_b = jnp.broadcast_to(chunk_decay[:, None, None], (h, 8, HEAD_DIM))

    def col(off):
        return pl.BlockSpec((c, HEAD_DIM), lambda hh, n: (n, off + hh))

    def per_head(shape):
        return pl.BlockSpec((1,) + shape, lambda hh, n: (hh, 0, 0))

    return pl.pallas_call(
        _retention_kernel,
        grid=(h, s // c),
        in_specs=[col(0), col(h), col(2 * h), col(3 * h),
                  per_head((c, c)), per_head((c, HEAD_DIM)), per_head((c, HEAD_DIM)),
                  per_head((8, HEAD_DIM)),
                  pl.BlockSpec((1, HEAD_DIM), lambda hh, n: (0, hh))],
        out_specs=pl.BlockSpec((c, HEAD_DIM), lambda hh, n: (n, hh)),
        out_shape=jax.ShapeDtypeStruct((s, h * HEAD_DIM), BF16),
        scratch_shapes=[pltpu.VMEM((HEAD_DIM, HEAD_DIM), F32)],
        compiler_params=_params(2),
        name="retention",
    )(zb, zb, zb, zb, intra, kw_b, qw_b, cd_b, ret_gn.reshape(1, -1).astype(F32))


def _forget_cumsum_kernel(fb_ref, fl_ref, o_ref):
    x = fl_ref[0] + fb_ref[pl.program_id(0)]
    logf = jnp.minimum(x, 0.0) - jnp.log1p(jnp.exp(-jnp.abs(x)))
    rows = x.shape[0]
    upper = (lax.broadcasted_iota(jnp.int32, (128, 128), 0)
             <= lax.broadcasted_iota(jnp.int32, (128, 128), 1)).astype(F32)
    within = _dot_exact(logf, upper)
    totals = jnp.broadcast_to(within[:, 127:128], (rows, 128))
    strict_lower = (lax.broadcasted_iota(jnp.int32, (rows, rows), 1)
                    < lax.broadcasted_iota(jnp.int32, (rows, rows), 0)).astype(F32)
    o_ref[0] = (within + _dot_exact(strict_lower, totals)) * LOG2E


def forget_cumsum_call(fl_t, fox_fb):
    h, s = fl_t.shape
    rows = s // 128
    out = pl.pallas_call(
        _forget_cumsum_kernel,
        grid=(h,),
        in_specs=[pl.BlockSpec(memory_space=pltpu.SMEM),
                  pl.BlockSpec((1, rows, 128), lambda i: (i, 0, 0))],
        out_specs=pl.BlockSpec((1, rows, 128), lambda i: (i, 0, 0)),
        out_shape=jax.ShapeDtypeStruct((h, rows, 128), F32),
        compiler_params=_params(1),
        name="forget_cumsum",
    )(fox_fb.astype(F32), fl_t.reshape(h, rows, 128))
    return out.reshape(h, s)


def _online_softmax_step(s, offset, v_ones, m_ref, acc_ref, idx, guard_empty_rows=False):
    m_old = m_ref[idx]
    m_new = jnp.maximum(m_old, jnp.max(s, -1, keepdims=True) + offset)
    m_fin = jnp.where(m_new == NEG_INF, 0.0, m_new) if guard_empty_rows else m_new
    alpha = jnp.exp2(m_old - m_fin)
    p = jnp.exp2(s - jnp.tile(m_fin - offset, (1, s.shape[1] // LANES)))
    acc_ref[idx] = jnp.tile(alpha, (1, 2)) * acc_ref[idx] + _dot(p.astype(BF16), v_ones)
    m_ref[idx] = m_new


def _init_softmax_state(m_ref, acc_ref):
    m_ref[...] = jnp.full(m_ref.shape, NEG_INF, F32)
    acc_ref[...] = jnp.zeros(acc_ref.shape, F32)


def _with_ones(v):
    return jnp.concatenate([v, jnp.ones(v.shape, v.dtype)], axis=1)


def _normalised(acc):
    return acc[:, :HEAD_DIM] / acc[:, HEAD_DIM:]


def _tile_rows_cols(t):
    return (lax.broadcasted_iota(jnp.int32, (t, t), 0),
            lax.broadcasted_iota(jnp.int32, (t, t), 1))


def _sweep_tiles_pipelined(n_tiles, scores_into, consume, buf0, buf1):
    @pl.when(n_tiles > 0)
    def _():
        scores_into(0, buf0)

    def pair(j, carry):
        first = 2 * j
        scores_into(first + 1, buf1)
        consume(first, buf0)
        scores_into(jnp.minimum(first + 2, n_tiles - 1), buf0)
        consume(first + 1, buf1)
        return carry

    lax.fori_loop(0, n_tiles // 2, pair, 0)

    @pl.when(n_tiles % 2 == 1)
    def _():
        consume(n_tiles - 1, buf0)


def _key_tile(ref, ki, t):
    return ref[pl.ds(pl.multiple_of(ki * t, t), t), :]


def _fox_kernel(q_ref, k_ref, v_ref, ccol_ref, crow_ref, o_ref, m_ref, acc_ref, buf0, buf1, *, t):
    qi = pl.program_id(1)
    q = q_ref[...]
    cq = jnp.broadcast_to(ccol_ref[0], (t, LANES))
    _init_softmax_state(m_ref, acc_ref)
    scale = HEAD_DIM ** -0.5 * LOG2E

    def scores_into(ki, buf):
        buf[0] = _dot_nt(q, _key_tile(k_ref, ki, t))

    def consume(ki, buf, causal=False):
        s = buf[0] * scale - crow_ref[0, ki]
        if causal:
            rows, cols = _tile_rows_cols(t)
            s = jnp.where(cols <= rows, s, NEG_INF)
        _online_softmax_step(s, cq, _with_ones(_key_tile(v_ref, ki, t)), m_ref, acc_ref, 0)

    _sweep_tiles_pipelined(qi, scores_into, consume, buf0, buf1)
    scores_into(qi, buf0)
    consume(qi, buf0, causal=True)
    o_ref[...] = _normalised(acc_ref[0]).astype(o_ref.dtype)


def fox_attention_call(zb, c, q_off, k_off, v_off, t):
    s = zb.shape[0]
    h = FOX_HEADS
    nt = s // t
    c_col = c.reshape(h, s, 1)
    c_row = c.reshape(h, nt, 1, t)
    return pl.pallas_call(
        functools.partial(_fox_kernel, t=t),
        grid=(h, nt),
        in_specs=[pl.BlockSpec((t, HEAD_DIM), lambda hh, i: (i, q_off + hh)),
                  pl.BlockSpec((s, HEAD_DIM), lambda hh, i: (0, k_off + hh)),
                  pl.BlockSpec((s, HEAD_DIM), lambda hh, i: (0, v_off + hh)),
                  pl.BlockSpec((1, t, 1), lambda hh, i: (hh, i, 0)),
                  pl.BlockSpec((1, nt, 1, t), lambda hh, i: (hh, 0, 0, 0))],
        out_specs=pl.BlockSpec((t, HEAD_DIM), lambda hh, i: (i, hh)),
        out_shape=jax.ShapeDtypeStruct((s, h * HEAD_DIM), BF16),
        scratch_shapes=[pltpu.VMEM((1, t, LANES), F32), pltpu.VMEM((1, t, 2 * HEAD_DIM), F32),
                        pltpu.VMEM((1, t, t), F32), pltpu.VMEM((1, t, t), F32)],
        compiler_params=_params(2),
        name="fox_attention",
    )(zb, zb, zb, c_col, c_row)


def _t5_bias(tbl_ref, head, dist):
    bias = jnp.full(dist.shape, tbl_ref[0, head], F32)
    for b in range(1, T5_BUCKETS):
        bias = jnp.where(dist >= T5_THRESH[b], tbl_ref[b, head], bias)
    return bias


def _fill_t5_tiles(tbl_ref, head, bias_ref, slot, t):
    rows, cols = _tile_rows_cols(t)
    dist = rows - cols
    bias_ref[slot, 0] = _t5_bias(tbl_ref, head, dist) * LOG2E
    bias_ref[slot, 1] = _t5_bias(tbl_ref, head, dist + t) * LOG2E


def _diff_kernel(tbl_ref, q_ref, k_ref, v_ref, lam_ref, g_ref, o_ref,
                 m_ref, acc_ref, bias_ref, buf0, buf1, *, t, lambda_init):
    head = pl.program_id(0)
    qi = pl.program_id(1)

    @pl.when(qi == 0)
    def _():
        _fill_t5_tiles(tbl_ref, head, bias_ref, 0, t)

    q = q_ref[...]
    lane = lax.broadcasted_iota(jnp.int32, q.shape, 1)
    zero = jnp.zeros_like(q)
    q_maps = (jnp.where(lane < HEAD_DIM // 2, q, zero), jnp.where(lane >= HEAD_DIM // 2, q, zero))
    _init_softmax_state(m_ref, acc_ref)
    scale = (HEAD_DIM // 2) ** -0.5 * LOG2E
    far_bias = tbl_ref[T5_BUCKETS - 1, head] * LOG2E

    def scores_into(ki, buf):
        k = _key_tile(k_ref, ki, t)
        for mi in range(2):
            buf[mi] = _dot_nt(q_maps[mi], k)

    def consume(ki, buf, kind="far"):
        v_ones = _with_ones(_key_tile(v_ref, ki, t))
        for mi in range(2):
            s = buf[mi] * scale
            if kind == "far":
                _online_softmax_step(s, far_bias, v_ones, m_ref, acc_ref, mi)
                continue
            s = s + bias_ref[0, 1 if kind == "near" else 0]
            if kind == "diag":
                rows, cols = _tile_rows_cols(t)
                s = jnp.where(cols <= rows, s, NEG_INF)
            _online_softmax_step(s, 0.0, v_ones, m_ref, acc_ref, mi)

    _sweep_tiles_pipelined(jnp.maximum(qi - 1, 0), scores_into, consume, buf0, buf1)

    @pl.when(qi >= 1)
    def _():
        scores_into(qi - 1, buf0)
        consume(qi - 1, buf0, "near")

    scores_into(qi, buf0)
    consume(qi, buf0, "diag")

    lam = lam_ref[...]
    lmbda = (jnp.exp(jnp.sum(lam[0:1] * lam[1:2], keepdims=True))
             - jnp.exp(jnp.sum(lam[2:3] * lam[3:4], keepdims=True)) + lambda_init)
    o = _normalised(acc_ref[0]) - lmbda * _normalised(acc_ref[1])
    y = o * lax.rsqrt(jnp.mean(o * o, -1, keepdims=True) + NORM_EPS) * g_ref[...]
    o_ref[...] = (y * (1.0 - lambda_init)).astype(o_ref.dtype)


def diff_attention_call(zb, t5_table, diff_lambda, subln, lambda_init, q_off, k_off, v_off, t):
    s = zb.shape[0]
    h = DIFF_HEADS
    nt = s // t
    return pl.pallas_call(
        functools.partial(_diff_kernel, t=t, lambda_init=lambda_init),
        grid=(h, nt),
        in_specs=[pl.BlockSpec(memory_space=pltpu.SMEM),
                  pl.BlockSpec((t, HEAD_DIM), lambda hh, i: (i, q_off + hh)),
                  pl.BlockSpec((s, HEAD_DIM), lambda hh, i: (0, k_off + hh)),
                  pl.BlockSpec((s, HEAD_DIM), lambda hh, i: (0, v_off + hh)),
                  pl.BlockSpec(diff_lambda.shape, lambda hh, i: (0, 0)),
                  pl.BlockSpec((1, HEAD_DIM), lambda hh, i: (0, 0))],
        out_specs=pl.BlockSpec((t, HEAD_DIM), lambda hh, i: (i, hh)),
        out_shape=jax.ShapeDtypeStruct((s, h * HEAD_DIM), BF16),
        scratch_shapes=[pltpu.VMEM((2, t, LANES), F32), pltpu.VMEM((2, t, 2 * HEAD_DIM), F32),
                        pltpu.VMEM((1, 2, t, t), F32),
                        pltpu.VMEM((2, t, t), F32), pltpu.VMEM((2, t, t), F32)],
        compiler_params=_params(2),
        name="diff_attention",
    )(t5_table.astype(F32), zb, zb, zb, diff_lambda.astype(F32), subln.reshape(1, -1).astype(F32))


def _compress_kernel(kr_ref, pos_ref, w1_ref, w2_ref, o_ref):
    half = w1_ref.shape[1] // 2
    kr = kr_ref[0, 0]
    w_top = w1_ref[0, :half, :]
    w_bot = w1_ref[0, half:, :]
    pos = pos_ref[0].astype(BF16)
    first = _dot(kr, w_top)
    second = _dot(kr, w_bot)
    pos_part = _dot(pos[:, :half], w_top) + _dot(pos[:, half:], w_bot)
    rows = kr.shape[0]
    second_next = pltpu.roll(second, rows - 1, 0)
    hid = first + second_next + pos_part[0:1, :]
    hid = hid * jax.nn.sigmoid(hid)
    o_ref[0, 0] = _dot(hid.astype(BF16), w2_ref[0]).astype(o_ref.dtype)


def compress_call(kv_rows, pos, w1, w2):
    two, g, nc, width = kv_rows.shape
    return pl.pallas_call(
        _compress_kernel,
        grid=(two, g),
        in_specs=[pl.BlockSpec((1, 1, nc, width), lambda a, b: (a, b, 0, 0)),
                  pl.BlockSpec((1, 16, 2 * width), lambda a, b: (a, 0, 0)),
                  pl.BlockSpec((1, 2 * width, HEAD_DIM), lambda a, b: (a, 0, 0)),
                  pl.BlockSpec((1, HEAD_DIM, HEAD_DIM), lambda a, b: (a, 0, 0))],
        out_specs=pl.BlockSpec((1, 1, nc, HEAD_DIM), lambda a, b: (a, b, 0, 0)),
        out_shape=jax.ShapeDtypeStruct((two, g, nc, HEAD_DIM), BF16),
        compiler_params=_params(2),
        name="nsa_compress",
    )(kv_rows, pos, w1, w2)


def _nsa_cmp_kernel(tbl_ref, q_ref, kc_ref, vc_ref, gate_ref, o_ref, sel_ref, *, head0):
    grp = pl.program_id(0)
    qb = pl.program_id(1)
    tq = q_ref.shape[0]
    ncp = kc_ref.shape[2]
    ns = sel_ref.shape[2]
    kc = kc_ref[0, 0]
    vc = vc_ref[0, 0]
    scale = HEAD_DIM ** -0.5

    t_pos = qb * tq + lax.broadcasted_iota(jnp.int32, (tq, ncp), 0)
    cmp_end = lax.broadcasted_iota(jnp.int32, (tq, ncp), 1) * CMP_STRIDE + (CMP_LEN - 1)
    visible = cmp_end <= t_pos

    band_rows = lax.broadcasted_iota(jnp.int32, (tq, tq), 0)
    band_lane = lax.broadcasted_iota(jnp.int32, (tq, tq), 1)
    band_dist = band_rows - (band_lane - CMP_BAND_BACK) * CMP_STRIDE - (CMP_LEN - 1)
    in_band = band_lane < CMP_BAND
    first_tok = qb * (tq // CMP_STRIDE) - CMP_BAND_BACK
    place = (lax.broadcasted_iota(jnp.int32, (tq, ncp), 1)
             == first_tok + lax.broadcasted_iota(jnp.int32, (tq, ncp), 0))
    place = jnp.where(place, 1.0, 0.0).astype(BF16)

    gates = jax.nn.sigmoid(gate_ref[0, 0])
    p_sum = jnp.zeros((tq, ncp), F32)
    for r in range(NSA_GROUP):
        head = head0 + grp * NSA_GROUP + r
        far = tbl_ref[T5_BUCKETS - 1, head]
        delta = jnp.where(in_band, _t5_bias(tbl_ref, head, band_dist) - far, 0.0)
        d_hi, d_mid, d_lo = _split_bf16(delta)
        bias = far + (_dot(d_hi, place) + _dot(d_mid, place) + _dot(d_lo, place))
        q = q_ref[:, r * HEAD_DIM:(r + 1) * HEAD_DIM]
        logits = jnp.where(visible, _dot_nt(q, kc) * scale + bias, NEG_INF)
        m = jnp.max(logits, -1, keepdims=True)
        m = jnp.where(m == NEG_INF, 0.0, m)
        e = jnp.where(visible, jnp.exp(logits - m), 0.0)
        p = e / jnp.maximum(jnp.sum(e, -1, keepdims=True), 1e-30)
        p_sum = p_sum + p
        o = _dot(p.astype(BF16), vc) * gates[:, r:r + 1]
        o_ref[:, r * HEAD_DIM:(r + 1) * HEAD_DIM] = o.astype(o_ref.dtype)

    ci = lax.broadcasted_iota(jnp.int32, (ncp, ns), 0) * CMP_STRIDE
    bj = lax.broadcasted_iota(jnp.int32, (ncp, ns), 1) * SLC_LEN
    overlap = jnp.where((ci <= bj + SLC_LEN - 1) & (ci + CMP_LEN - 1 >= bj), 1.0, 0.0).astype(BF16)
    p_hi, p_mid, p_lo = _split_bf16(p_sum)
    imp = _dot(p_hi, overlap) + _dot(p_mid, overlap) + _dot(p_lo, overlap)

    t_sel = qb * tq + lax.broadcasted_iota(jnp.int32, (tq, ns), 0)
    blk = lax.broadcasted_iota(jnp.int32, (tq, ns), 1)
    cur = t_sel // SLC_LEN
    forced = (blk == 0) | (blk == cur) | (blk == cur - 1)
    valid = blk * SLC_LEN <= t_sel
    score = jnp.where(forced, SEL_BIG, jnp.where(valid, imp, -SEL_BIG))
    blk_f = blk.astype(F32)

    def pick(_, carry):
        sc, chosen = carry
        best = jnp.max(sc, -1, keepdims=True)
        first = jnp.min(jnp.where(sc == best, blk_f, float(ns)), -1, keepdims=True)
        hit = blk_f == first
        return jnp.where(hit, NEG_INF, sc), jnp.where(hit, 1.0, chosen)

    _, chosen = lax.fori_loop(0, min(SLC_TOPN, ns), pick, (score, jnp.zeros((tq, ns), F32)))
    sel_ref[0] = chosen.astype(sel_ref.dtype)


def nsa_cmp_call(zb, kvc, gates, t5_table, q_off, ns):
    s = zb.shape[0]
    g = NSA_KV_HEADS
    tq = CMP_QBLOCK
    ncp = kvc.shape[2]
    width = NSA_GROUP * HEAD_DIM
    return pl.pallas_call(
        functools.partial(_nsa_cmp_kernel, head0=DIFF_HEADS),
        grid=(g, s // tq),
        in_specs=[pl.BlockSpec(memory_space=pltpu.SMEM),
                  pl.BlockSpec((tq, width), lambda gg, i: (i, q_off + gg)),
                  pl.BlockSpec((1, 1, ncp, HEAD_DIM), lambda gg, i: (0, gg, 0, 0)),
                  pl.BlockSpec((1, 1, ncp, HEAD_DIM), lambda gg, i: (1, gg, 0, 0)),
                  pl.BlockSpec((1, 1, tq, NSA_GROUP), lambda gg, i: (0, gg, i, 0))],
        out_specs=[pl.BlockSpec((tq, width), lambda gg, i: (i, gg)),
                   pl.BlockSpec((1, tq, ns), lambda gg, i: (gg, i, 0))],
        out_shape=[jax.ShapeDtypeStruct((s, g * width), BF16),
                   jax.ShapeDtypeStruct((g, s, ns), BF16)],
        compiler_params=_params(2),
        name="nsa_compressed",
    )(t5_table.astype(F32), zb, kvc, kvc, gates)


def _nsa_sweep_kernel(tbl_ref, q_ref, k_ref, v_ref, gate_ref, *rest, t, mode, head0, branch):
    if mode == "selected":
        sel_ref, o_ref, m_ref, acc_ref, bias_ref, buf0, buf1 = rest
    else:
        o_ref, m_ref, acc_ref, bias_ref, buf0, buf1 = rest
    grp = pl.program_id(0)
    qi = pl.program_id(1)

    @pl.when(qi == 0)
    def _():
        for r in range(NSA_GROUP):
            _fill_t5_tiles(tbl_ref, head0 + grp * NSA_GROUP + r, bias_ref, r, t)

    _init_softmax_state(m_ref, acc_ref)
    scale = HEAD_DIM ** -0.5 * LOG2E
    rows, cols = _tile_rows_cols(t)

    def scores_into(ki, buf):
        k = _key_tile(k_ref, ki, t)
        for r in range(NSA_GROUP):
            buf[r] = _dot_nt(q_ref[:, r * HEAD_DIM:(r + 1) * HEAD_DIM], k)
        if mode == "selected":
            ns = sel_ref.shape[2]
            blk_of_key = (ki * t + lax.broadcasted_iota(jnp.int32, (ns, t), 1)) // SLC_LEN
            expand = jnp.where(lax.broadcasted_iota(jnp.int32, (ns, t), 0) == blk_of_key, 1.0, 0.0)
            buf[NSA_GROUP] = _dot(sel_ref[0], expand.astype(BF16))

    def consume(ki, buf, kind="far"):
        v_ones = _with_ones(_key_tile(v_ref, ki, t))
        if mode == "selected":
            keep = buf[NSA_GROUP] > 0.5
            if kind == "diag":
                keep = keep & (cols <= rows)
        else:
            keep = (cols <= rows) if kind == "diag" else (cols > rows)
        for r in range(NSA_GROUP):
            s = buf[r] * scale
            if kind == "far":
                far_bias = tbl_ref[T5_BUCKETS - 1, head0 + grp * NSA_GROUP + r] * LOG2E
                _online_softmax_step(jnp.where(keep, s, NEG_INF), far_bias, v_ones, m_ref, acc_ref, r)
                continue
            s = jnp.where(keep, s + bias_ref[r, 1 if kind == "near" else 0], NEG_INF)
            _online_softmax_step(s, 0.0, v_ones, m_ref, acc_ref, r,
                                 guard_empty_rows=(mode == "window"))

    if mode == "selected":
        _sweep_tiles_pipelined(jnp.maximum(qi - 1, 0), scores_into, consume, buf0, buf1)

    @pl.when(qi >= 1)
    def _():
        scores_into(qi - 1, buf0)
        consume(qi - 1, buf0, "near")

    scores_into(qi, buf1)
    consume(qi, buf1, "diag")

    gates = jax.nn.sigmoid(gate_ref[0, 0])
    for r in range(NSA_GROUP):
        o = _normalised(acc_ref[r]) * gates[:, r:r + 1]
        o_ref[:, r * HEAD_DIM:(r + 1) * HEAD_DIM] = o.astype(o_ref.dtype)


def nsa_sweep_call(zb, gates, t5_table, sel, q_off, k_off, v_off, t, mode):
    s = zb.shape[0]
    g = NSA_KV_HEADS
    nt = s // t
    width = NSA_GROUP * HEAD_DIM
    branch = 1 if mode == "selected" else 2
    n_buf = NSA_GROUP + 1 if mode == "selected" else NSA_GROUP
    if mode == "window":
        assert t == WINDOW
    in_specs = [pl.BlockSpec(memory_space=pltpu.SMEM),
                pl.BlockSpec((t, width), lambda gg, i: (i, q_off + gg)),
                pl.BlockSpec((s, HEAD_DIM), lambda gg, i: (0, k_off + gg)),
                pl.BlockSpec((s, HEAD_DIM), lambda gg, i: (0, v_off + gg)),
                pl.BlockSpec((1, 1, t, NSA_GROUP), lambda gg, i: (branch, gg, i, 0))]
    args = [t5_table.astype(F32), zb, zb, zb, gates]
    if mode == "selected":
        ns = sel.shape[2]
        in_specs.append(pl.BlockSpec((1, t, ns), lambda gg, i: (gg, i, 0)))
        args.append(sel)
    return pl.pallas_call(
        functools.partial(_nsa_sweep_kernel, t=t, mode=mode, head0=DIFF_HEADS, branch=branch),
        grid=(g, nt),
        in_specs=in_specs,
        out_specs=pl.BlockSpec((t, width), lambda gg, i: (i, gg)),
        out_shape=jax.ShapeDtypeStruct((s, g * width), BF16),
        scratch_shapes=[pltpu.VMEM((NSA_GROUP, t, LANES), F32),
                        pltpu.VMEM((NSA_GROUP, t, 2 * HEAD_DIM), F32),
                        pltpu.VMEM((NSA_GROUP, 2, t, t), F32),
                        pltpu.VMEM((n_buf, t, t), F32), pltpu.VMEM((n_buf, t, t), F32)],
        compiler_params=_params(2),
        name="nsa_" + mode,
    )(*args)


def _pad_cols(w, n):
    return jnp.pad(w, ((0, 0), (0, n - w.shape[1])))


def even_mixer(x, norm_g, w_in, ret_gn, fox_fb, w_out, t):
    s = x.shape[0]
    hb = rmsnorm_call(x, norm_g, BF16)
    main = 7 * RET_HEADS * HEAD_DIM
    zb = matmul_call(hb, w_in[:, :main].astype(BF16), BF16)
    zg = matmul_call(hb, _pad_cols(w_in[:, main:], HEAD_DIM).astype(BF16), F32, tn=HEAD_DIM)
    ret_out = retention_call(zb, ret_gn)
    c = forget_cumsum_call(zg[:, :FOX_HEADS].T, fox_fb)
    fox_out = fox_attention_call(zb, c, 4 * RET_HEADS, 5 * RET_HEADS, 6 * RET_HEADS, t)
    half = RET_HEADS * HEAD_DIM
    w_out_b = w_out.astype(BF16)
    return residual_matmul_call(x, [ret_out], w_out_b[:half], [fox_out], w_out_b[half:])


def odd_mixer(x, norm_g, w_in, diff_lambda, diff_subln, cmp_pos, cmp_w1, cmp_w2, w_out, t5_table,
              lambda_init, t):
    s = x.shape[0]
    hb = rmsnorm_call(x, norm_g, BF16)
    main = 5632
    zb = matmul_call(hb, w_in[:, :main].astype(BF16), BF16)
    zg = matmul_call(hb, _pad_cols(w_in[:, main:], HEAD_DIM).astype(BF16), F32, tn=HEAD_DIM)
    hd = HEAD_DIM
    diff_out = diff_attention_call(zb, t5_table, diff_lambda, diff_subln, lambda_init, 0, 8, 16, t)
    g = NSA_KV_HEADS
    ckv = zb[:, 32 * hd:36 * hd].reshape(s // CMP_STRIDE, CMP_STRIDE, 2, g, hd)
    kv_rows = ckv.transpose(2, 3, 0, 1, 4).reshape(2, g, s // CMP_STRIDE, CMP_STRIDE * hd)
    pos = jnp.broadcast_to(cmp_pos.reshape(2, 1, CMP_LEN * hd), (2, 16, CMP_LEN * hd)).astype(F32)
    kvc = compress_call(kv_rows, pos, cmp_w1.astype(BF16), cmp_w2.astype(BF16))
    gates = zg[:, :3 * NSA_HEADS].reshape(s, 3, g, NSA_GROUP).transpose(1, 2, 0, 3)
    ns = s // SLC_LEN
    o_cmp, sel = nsa_cmp_call(zb, kvc, gates, t5_table, 6, ns)
    o_slc = nsa_sweep_call(zb, gates, t5_table, sel, 6, 36, 38, t, "selected")
    o_win = nsa_sweep_call(zb, gates, t5_table, None, 6, 40, 42, WINDOW, "window")
    half = DIFF_HEADS * hd
    w_out_b = w_out.astype(BF16)
    return residual_matmul_call(x, [diff_out], w_out_b[:half], [o_cmp, o_slc, o_win], w_out_b[half:])


def ffn_and_ple(x, norm_g, w_gate, w_up, w_down, p, ple_gate, ple_proj):
    hb = rmsnorm_call(x, norm_g, BF16)
    u = swiglu_call(hb, w_gate.astype(BF16), w_up.astype(BF16))
    x2, x2b = residual_matmul_call(x, [u], w_down.astype(BF16), with_bf16=True)
    return ple_call(x2, x2b, ple_gate.astype(BF16), p, ple_proj.astype(BF16))


def _attention_tile(s):
    return min(512, s)


def kernel(x, p, norm_mix, norm_ffn, w_in_even, ret_gn, fox_fb, w_out_even, w_in_odd, diff_lambda, diff_subln, cmp_pos, cmp_w1, cmp_w2, w_out_odd, t5_table, ffn_gate, ffn_up, ffn_down, ple_gate, ple_proj, final_norm):
    batch, s, d = x.shape
    assert batch == 1
    depth = p.shape[0]
    t = _attention_tile(s)
    xs = x[0]
    for i in range(depth):
        j = i // 2
        if i % 2 == 0:
            xs = even_mixer(xs, norm_mix[i], w_in_even[j], ret_gn[j], fox_fb[j], w_out_even[j], t)
        else:
            lambda_init = 0.8 - 0.6 * math.exp(-0.3 * i)
            xs = odd_mixer(xs, norm_mix[i], w_in_odd[j], diff_lambda[j], diff_subln[j], cmp_pos[j],
                           cmp_w1[j], cmp_w2[j], w_out_odd[j], t5_table, lambda_init, t)
        xs = ffn_and_ple(xs, norm_ffn[i], ffn_gate[i], ffn_up[i], ffn_down[i], p[i, 0],
                         ple_gate[i], ple_proj[i])
    out = rmsnorm_call(xs, final_norm, F32)
    return out[None]
```

```python
import functools
import math

import numpy as np
import jax
import jax.numpy as jnp
from jax import lax
from jax.experimental import pallas as pl
from jax.experimental.pallas import tpu as pltpu

F32 = jnp.float32
BF16 = jnp.bfloat16

HEAD_DIM = 128
NORM_EPS = 1e-6
RET_HEADS = 8
FOX_HEADS = 8
DIFF_HEADS = 8
NSA_HEADS = 8
NSA_KV_HEADS = 2
NSA_GROUP = NSA_HEADS // NSA_KV_HEADS
RET_CHUNK = 128
CMP_LEN = 32
CMP_STRIDE = 16
SLC_LEN = 64
SLC_TOPN = 16
WINDOW = 512
SEL_BIG = 1e9
T5_BUCKETS = 32
T5_MAX_DIST = 128
PLE_DIM = 256
CMP_QBLOCK = 128
CMP_BAND = 16
CMP_BAND_BACK = 9

_T5_EXACT = T5_BUCKETS // 2
T5_THRESH = tuple(
    b if b <= _T5_EXACT else int(math.ceil(
        _T5_EXACT * (T5_MAX_DIST / _T5_EXACT) ** ((b - _T5_EXACT) / (T5_BUCKETS - _T5_EXACT))))
    for b in range(T5_BUCKETS))

VMEM_LIMIT = 56 * 1024 * 1024
NEG_INF = float("-inf")
LANES = 128
SWEEP_UNROLL = 4
RET_CHUNKS_PER_STEP = 4
LOG2E = math.log2(math.e)


def _params(n_axes):
    return pltpu.CompilerParams(dimension_semantics=("arbitrary",) * n_axes,
                                vmem_limit_bytes=VMEM_LIMIT)


def _dot(a, b):
    return jnp.dot(a, b, preferred_element_type=F32)


def _dot_nt(a, b):
    return lax.dot_general(a, b, (((1,), (1,)), ((), ())), preferred_element_type=F32)


def _dot_exact(a, b):
    return jnp.dot(a, b, preferred_element_type=F32, precision=lax.Precision.HIGHEST)


def _split_bf16(x):
    hi = x.astype(BF16)
    r1 = x - hi.astype(F32)
    mid = r1.astype(BF16)
    lo = (r1 - mid.astype(F32)).astype(BF16)
    return hi, mid, lo


def _rmsnorm_kernel(x_ref, g_ref, o_ref):
    x = x_ref[...]
    y = x * lax.rsqrt(jnp.mean(x * x, -1, keepdims=True) + NORM_EPS) * g_ref[...]
    o_ref[...] = y.astype(o_ref.dtype)


def rmsnorm_call(x, g, out_dtype, tm=512):
    m, d = x.shape
    tm = min(tm, m)
    return pl.pallas_call(
        _rmsnorm_kernel,
        grid=(m // tm,),
        in_specs=[pl.BlockSpec((tm, d), lambda i: (i, 0)),
                  pl.BlockSpec((1, d), lambda i: (0, 0))],
        out_specs=pl.BlockSpec((tm, d), lambda i: (i, 0)),
        out_shape=jax.ShapeDtypeStruct((m, d), out_dtype),
        compiler_params=_params(1),
        name="rmsnorm",
    )(x, g.reshape(1, d).astype(F32))


def _mm_plain_kernel(a_ref, w_ref, o_ref):
    o_ref[...] = _dot(a_ref[...], w_ref[...].astype(BF16)).astype(o_ref.dtype)


def _weight_spec(k, tn, layer, row_block=0):
    return pl.BlockSpec((None, k, tn), lambda i, j: (layer, row_block, j))


def matmul_call(a, w, layer, out_dtype, n=None, tm=1024, tn=512):
    m, k = a.shape
    n = w.shape[2] if n is None else n
    tm, tn = min(tm, m), min(tn, n)
    return pl.pallas_call(
        _mm_plain_kernel,
        grid=(m // tm, n // tn),
        in_specs=[pl.BlockSpec((tm, k), lambda i, j: (i, 0)),
                  _weight_spec(k, tn, layer)],
        out_specs=pl.BlockSpec((tm, tn), lambda i, j: (i, j)),
        out_shape=jax.ShapeDtypeStruct((m, n), out_dtype),
        compiler_params=_params(2),
        name="matmul",
    )(a, w)


def _mm_swiglu_kernel(a_ref, wg_ref, wu_ref, o_ref):
    a = a_ref[...]
    g = _dot(a, wg_ref[...].astype(BF16))
    u = _dot(a, wu_ref[...].astype(BF16))
    o_ref[...] = (g * jax.nn.sigmoid(g) * u).astype(o_ref.dtype)


def swiglu_call(a, wg, wu, layer, tm=1024, tn=512):
    m, k = a.shape
    n = wg.shape[2]
    tm, tn = min(tm, m), min(tn, n)
    return pl.pallas_call(
        _mm_swiglu_kernel,
        grid=(m // tm, n // tn),
        in_specs=[pl.BlockSpec((tm, k), lambda i, j: (i, 0)),
                  _weight_spec(k, tn, layer),
                  _weight_spec(k, tn, layer)],
        out_specs=pl.BlockSpec((tm, tn), lambda i, j: (i, j)),
        out_shape=jax.ShapeDtypeStruct((m, n), BF16),
        compiler_params=_params(2),
        name="swiglu",
    )(a, wg, wu)


def _mm_residual_kernel(*refs, n_a, n_b, with_bf16):
    res_ref = refs[0]
    a_refs = refs[1:1 + n_a]
    w1_ref = refs[1 + n_a]
    pos = 2 + n_a
    acc = res_ref[...]

    def summed(group):
        if len(group) == 1:
            return group[0][...]
        tot = group[0][...].astype(F32)
        for r in group[1:]:
            tot = tot + r[...].astype(F32)
        return tot.astype(BF16)

    acc = acc + _dot(summed(a_refs), w1_ref[...].astype(BF16))
    if n_b:
        b_refs = refs[pos:pos + n_b]
        w2_ref = refs[pos + n_b]
        pos += n_b + 1
        acc = acc + _dot(summed(b_refs), w2_ref[...].astype(BF16))
    refs[pos][...] = acc
    if with_bf16:
        refs[pos + 1][...] = acc.astype(BF16)


def residual_matmul_call(res, a_list, w, layer, b_list=(), with_bf16=False, tm=1024, tn=512):
    m, n = res.shape
    tm, tn = min(tm, m), min(tn, n)
    ka = a_list[0].shape[1]
    args = [res] + list(a_list) + [w]
    in_specs = [pl.BlockSpec((tm, tn), lambda i, j: (i, j))]
    in_specs += [pl.BlockSpec((tm, ka), lambda i, j: (i, 0)) for _ in a_list]
    in_specs += [_weight_spec(ka, tn, layer)]
    if b_list:
        kb = b_list[0].shape[1]
        assert kb == ka and w.shape[1] == ka + kb
        args += list(b_list) + [w]
        in_specs += [pl.BlockSpec((tm, kb), lambda i, j: (i, 0)) for _ in b_list]
        in_specs += [_weight_spec(kb, tn, layer, row_block=1)]
    else:
        assert w.shape[1] == ka
    out_shape = [jax.ShapeDtypeStruct((m, n), F32)]
    out_specs = [pl.BlockSpec((tm, tn), lambda i, j: (i, j))]
    if with_bf16:
        out_shape.append(jax.ShapeDtypeStruct((m, n), BF16))
        out_specs.append(pl.BlockSpec((tm, tn), lambda i, j: (i, j)))
    out = pl.pallas_call(
        functools.partial(_mm_residual_kernel, n_a=len(a_list), n_b=len(b_list), with_bf16=with_bf16),
        grid=(m // tm, n // tn),
        in_specs=in_specs,
        out_specs=out_specs,
        out_shape=out_shape,
        compiler_params=_params(2),
        name="residual_matmul",
    )(*args)
    return out if with_bf16 else out[0]


def _ple_kernel(xb_ref, xres_ref, wg_ref, p_ref, wp_ref, o_ref):
    gate = jax.nn.sigmoid(_dot(xb_ref[...], wg_ref[...].astype(BF16)))
    emb = _dot(p_ref[...].astype(BF16), wp_ref[...].astype(BF16))
    o_ref[...] = xres_ref[...] + gate * emb


def ple_call(x, xb, wg, p, wp, layer, tm=1024, tn=512):
    m, n = x.shape
    tm, tn = min(tm, m), min(tn, n)
    k = xb.shape[1]
    kp = p.shape[3]
    return pl.pallas_call(
        _ple_kernel,
        grid=(m // tm, n // tn),
        in_specs=[pl.BlockSpec((tm, k), lambda i, j: (i, 0)),
                  pl.BlockSpec((tm, tn), lambda i, j: (i, j)),
                  _weight_spec(k, tn, layer),
                  pl.BlockSpec((None, None, tm, kp), lambda i, j: (layer, 0, i, 0)),
                  _weight_spec(kp, tn, layer)],
        out_specs=pl.BlockSpec((tm, tn), lambda i, j: (i, j)),
        out_shape=jax.ShapeDtypeStruct((m, n), F32),
        compiler_params=_params(2),
        name="ple",
    )(xb, x, wg, p, wp)


def _retention_kernel(q_ref, k_ref, v_ref, g_ref, dec_ref, kw_ref, qw_ref, cd_ref, gn_ref,
                      o_ref, state_ref):
    @pl.when(pl.program_id(1) == 0)
    def _():
        state_ref[...] = jnp.zeros_like(state_ref)

    c = RET_CHUNK
    state = state_ref[...]
    for i in range(q_ref.shape[0] // c):
        rows = slice(i * c, (i + 1) * c)
        q = q_ref[rows, :]
        kf = k_ref[rows, :].astype(F32) * (HEAD_DIM ** -0.5)
        v = v_ref[rows, :]
        att = _dot_nt(q, kf.astype(BF16)) * dec_ref[0]
        y = _dot(att.astype(BF16), v)
        q_scaled = (q.astype(F32) * qw_ref[0]).astype(BF16)
        y = y + _dot(q_scaled, state.astype(BF16))
        k_scaled_t = (kf * kw_ref[0]).T.astype(BF16)
        state = state * cd_ref[0][0:1, :] + _dot(k_scaled_t, v)
        mu = jnp.mean(y, -1, keepdims=True)
        yc = y - mu
        var = jnp.mean(yc * yc, -1, keepdims=True)
        yn = yc * lax.rsqrt(var + NORM_EPS) * gn_ref[...]
        g = g_ref[rows, :].astype(F32)
        o_ref[rows, :] = (yn * (g * jax.nn.sigmoid(g))).astype(o_ref.dtype)
    state_ref[...] = state


def retention_call(zb, ret_gn):
    s = zb.shape[0]
    c = RET_CHUNK
    h = RET_HEADS
    log_gamma = jnp.log1p(-jnp.exp2(-5.0 - jnp.arange(h, dtype=F32)))
    pos = jnp.arange(c, dtype=F32)
    rel = pos[:, None] - pos[None, :]
    intra = jnp.where(rel >= 0, jnp.exp(log_gamma[:, None, None] * jnp.maximum(rel, 0.0)), 0.0)
    k_w = jnp.exp(log_gamma[:, None] * (c - 1 - pos))
    q_w = jnp.exp(log_gamma[:, None] * (pos + 1.0))
    chunk_decay = jnp.exp(log_gamma * c)
    kw_b = jnp.broadcast_to(k_w[:, :, None], (h, c, HEAD_DIM))
    qw_b = jnp.broadcast_to(q_w[:, :, None], (h, c, HEAD_DIM))
    cd_b = jnp.broadcast_to(chunk_decay[:, None, None], (h, 8, HEAD_DIM))

    rows = c * RET_CHUNKS_PER_STEP if s % (c * RET_CHUNKS_PER_STEP) == 0 else c

    def col(off):
        return pl.BlockSpec((rows, HEAD_DIM), lambda hh, n: (n, off + hh))

    def per_head(shape):
        return pl.BlockSpec((1,) + shape, lambda hh, n: (hh, 0, 0))

    return pl.pallas_call(
        _retention_kernel,
        grid=(h, s // rows),
        in_specs=[col(0), col(h), col(2 * h), col(3 * h),
                  per_head((c, c)), per_head((c, HEAD_DIM)), per_head((c, HEAD_DIM)),
                  per_head((8, HEAD_DIM)),
                  pl.BlockSpec((1, HEAD_DIM), lambda hh, n: (0, hh))],
        out_specs=pl.BlockSpec((rows, HEAD_DIM), lambda hh, n: (n, hh)),
        out_shape=jax.ShapeDtypeStruct((s, h * HEAD_DIM), BF16),
        scratch_shapes=[pltpu.VMEM((HEAD_DIM, HEAD_DIM), F32)],
        compiler_params=_params(2),
        name="retention",
    )(zb, zb, zb, zb, intra, kw_b, qw_b, cd_b, ret_gn.reshape(1, -1).astype(F32))


def _forget_cumsum_kernel(fb_ref, fl_ref, o_ref):
    x = fl_ref[0] + fb_ref[pl.program_id(0)]
    logf = jnp.minimum(x, 0.0) - jnp.log1p(jnp.exp(-jnp.abs(x)))
    rows = x.shape[0]
    upper = (lax.broadcasted_iota(jnp.int32, (128, 128), 0)
             <= lax.broadcasted_iota(jnp.int32, (128, 128), 1)).astype(F32)
    within = _dot_exact(logf, upper)
    totals = jnp.broadcast_to(within[:, 127:128], (rows, 128))
    strict_lower = (lax.broadcasted_iota(jnp.int32, (rows, rows), 1)
                    < lax.broadcasted_iota(jnp.int32, (rows, rows), 0)).astype(F32)
    o_ref[0] = (within + _dot_exact(strict_lower, totals)) * LOG2E


def forget_cumsum_call(fl_t, fox_fb):
    h, s = fl_t.shape
    rows = s // 128
    out = pl.pallas_call(
        _forget_cumsum_kernel,
        grid=(h,),
        in_specs=[pl.BlockSpec(memory_space=pltpu.SMEM),
                  pl.BlockSpec((1, rows, 128), lambda i: (i, 0, 0))],
        out_specs=pl.BlockSpec((1, rows, 128), lambda i: (i, 0, 0)),
        out_shape=jax.ShapeDtypeStruct((h, rows, 128), F32),
        compiler_params=_params(1),
        name="forget_cumsum",
    )(fox_fb.astype(F32), fl_t.reshape(h, rows, 128))
    return out.reshape(h, s)


def _online_softmax_step(s, offset, v_ones, m_ref, acc_ref, idx, guard_empty_rows=False):
    m_old = m_ref[idx]
    m_new = jnp.maximum(m_old, jnp.max(s, -1, keepdims=True) + offset)
    m_fin = jnp.where(m_new == NEG_INF, 0.0, m_new) if guard_empty_rows else m_new
    alpha = jnp.exp2(m_old - m_fin)
    p = jnp.exp2(s - jnp.tile(m_fin - offset, (1, s.shape[1] // LANES)))
    acc_ref[idx] = jnp.tile(alpha, (1, 2)) * acc_ref[idx] + _dot(p.astype(BF16), v_ones)
    m_ref[idx] = m_new


def _init_softmax_state(m_ref, acc_ref):
    m_ref[...] = jnp.full(m_ref.shape, NEG_INF, F32)
    acc_ref[...] = jnp.zeros(acc_ref.shape, F32)


def _with_ones(v):
    return jnp.concatenate([v, jnp.ones(v.shape, v.dtype)], axis=1)


def _normalised(acc):
    return acc[:, :HEAD_DIM] / acc[:, HEAD_DIM:]


def _tile_rows_cols(t):
    return (lax.broadcasted_iota(jnp.int32, (t, t), 0),
            lax.broadcasted_iota(jnp.int32, (t, t), 1))


def _sweep_tiles_pipelined(n_tiles, scores_into, consume, buf0, buf1):
    bufs = (buf0, buf1)
    last = n_tiles - 1

    @pl.when(n_tiles > 0)
    def _():
        scores_into(0, buf0)

    def run(first, count):
        for u in range(count):
            scores_into(jnp.minimum(first + u + 1, last), bufs[(u + 1) % 2])
            consume(first + u, bufs[u % 2])

    def group(j, carry):
        run(SWEEP_UNROLL * j, SWEEP_UNROLL)
        return carry

    lax.fori_loop(0, n_tiles // SWEEP_UNROLL, group, 0)
    rest = n_tiles % SWEEP_UNROLL
    step = SWEEP_UNROLL // 2
    while step >= 2:
        taken = rest - rest % (2 * step)

        @pl.when(rest % (2 * step) >= step)
        def _(first=n_tiles - rest + taken, count=step):
            run(first, count)

        step //= 2

    @pl.when(rest % 2 == 1)
    def _():
        consume(last, buf0)


def _key_tile(ref, ki, t):
    return ref[pl.ds(pl.multiple_of(ki * t, t), t), :]


def _fox_kernel(q_ref, k_ref, v_ref, ccol_ref, crow_ref, o_ref, m_ref, acc_ref, buf0, buf1, *, t):
    qi = pl.program_id(1)
    q = q_ref[...]
    cq = jnp.broadcast_to(ccol_ref[0], (t, LANES))
    _init_softmax_state(m_ref, acc_ref)
    scale = HEAD_DIM ** -0.5 * LOG2E

    def scores_into(ki, buf):
        buf[0] = _dot_nt(q, _key_tile(k_ref, ki, t))

    def consume(ki, buf, causal=False):
        s = buf[0] * scale - crow_ref[0, ki]
        if causal:
            rows, cols = _tile_rows_cols(t)
            s = jnp.where(cols <= rows, s, NEG_INF)
        _online_softmax_step(s, cq, _with_ones(_key_tile(v_ref, ki, t)), m_ref, acc_ref, 0)

    _sweep_tiles_pipelined(qi, scores_into, consume, buf0, buf1)
    scores_into(qi, buf0)
    consume(qi, buf0, causal=True)
    o_ref[...] = _normalised(acc_ref[0]).astype(o_ref.dtype)


def fox_attention_call(zb, c, q_off, k_off, v_off, t):
    s = zb.shape[0]
    h = FOX_HEADS
    nt = s // t
    c_col = c.reshape(h, s, 1)
    c_row = c.reshape(h, nt, 1, t)
    return pl.pallas_call(
        functools.partial(_fox_kernel, t=t),
        grid=(h, nt),
        in_specs=[pl.BlockSpec((t, HEAD_DIM), lambda hh, i: (i, q_off + hh)),
                  pl.BlockSpec((s, HEAD_DIM), lambda hh, i: (0, k_off + hh)),
                  pl.BlockSpec((s, HEAD_DIM), lambda hh, i: (0, v_off + hh)),
                  pl.BlockSpec((1, t, 1), lambda hh, i: (hh, i, 0)),
                  pl.BlockSpec((1, nt, 1, t), lambda hh, i: (hh, 0, 0, 0))],
        out_specs=pl.BlockSpec((t, HEAD_DIM), lambda hh, i: (i, hh)),
        out_shape=jax.ShapeDtypeStruct((s, h * HEAD_DIM), BF16),
        scratch_shapes=[pltpu.VMEM((1, t, LANES), F32), pltpu.VMEM((1, t, 2 * HEAD_DIM), F32),
                        pltpu.VMEM((1, t, t), F32), pltpu.VMEM((1, t, t), F32)],
        compiler_params=_params(2),
        name="fox_attention",
    )(zb, zb, zb, c_col, c_row)


def _t5_bias(tbl_ref, head, dist):
    bias = jnp.full(dist.shape, tbl_ref[0, head], F32)
    for b in range(1, T5_BUCKETS):
        bias = jnp.where(dist >= T5_THRESH[b], tbl_ref[b, head], bias)
    return bias


def _fill_t5_tiles(tbl_ref, head, bias_ref, slot, t):
    rows, cols = _tile_rows_cols(t)
    dist = rows - cols
    bias_ref[slot, 0] = _t5_bias(tbl_ref, head, dist) * LOG2E
    bias_ref[slot, 1] = _t5_bias(tbl_ref, head, dist + t) * LOG2E


def _diff_kernel(tbl_ref, q_ref, k_ref, v_ref, lam_ref, g_ref, o_ref,
                 m_ref, acc_ref, bias_ref, buf0, buf1, *, t, lambda_init):
    head = pl.program_id(0)
    qi = pl.program_id(1)

    @pl.when(qi == 0)
    def _():
        _fill_t5_tiles(tbl_ref, head, bias_ref, 0, t)

    q = q_ref[...]
    lane = lax.broadcasted_iota(jnp.int32, q.shape, 1)
    zero = jnp.zeros_like(q)
    q_maps = (jnp.where(lane < HEAD_DIM // 2, q, zero), jnp.where(lane >= HEAD_DIM // 2, q, zero))
    _init_softmax_state(m_ref, acc_ref)
    scale = (HEAD_DIM // 2) ** -0.5 * LOG2E
    far_bias = tbl_ref[T5_BUCKETS - 1, head] * LOG2E

    def scores_into(ki, buf):
        k = _key_tile(k_ref, ki, t)
        for mi in range(2):
            buf[mi] = _dot_nt(q_maps[mi], k)

    def consume(ki, buf, kind="far"):
        v_ones = _with_ones(_key_tile(v_ref, ki, t))
        for mi in range(2):
            s = buf[mi] * scale
            if kind == "far":
                _online_softmax_step(s, far_bias, v_ones, m_ref, acc_ref, mi)
                continue
            s = s + bias_ref[0, 1 if kind == "near" else 0]
            if kind == "diag":
                rows, cols = _tile_rows_cols(t)
                s = jnp.where(cols <= rows, s, NEG_INF)
            _online_softmax_step(s, 0.0, v_ones, m_ref, acc_ref, mi)

    _sweep_tiles_pipelined(jnp.maximum(qi - 1, 0), scores_into, consume, buf0, buf1)

    @pl.when(qi >= 1)
    def _():
        scores_into(qi - 1, buf0)
        consume(qi - 1, buf0, "near")

    scores_into(qi, buf0)
    consume(qi, buf0, "diag")

    lam = lam_ref[...]
    lmbda = (jnp.exp(jnp.sum(lam[0:1] * lam[1:2], keepdims=True))
             - jnp.exp(jnp.sum(lam[2:3] * lam[3:4], keepdims=True)) + lambda_init)
    o = _normalised(acc_ref[0]) - lmbda * _normalised(acc_ref[1])
    y = o * lax.rsqrt(jnp.mean(o * o, -1, keepdims=True) + NORM_EPS) * g_ref[...]
    o_ref[...] = (y * (1.0 - lambda_init)).astype(o_ref.dtype)


def diff_attention_call(zb, t5_table, diff_lambda, subln, lambda_init, q_off, k_off, v_off, t):
    s = zb.shape[0]
    h = DIFF_HEADS
    nt = s // t
    return pl.pallas_call(
        functools.partial(_diff_kernel, t=t, lambda_init=lambda_init),
        grid=(h, nt),
        in_specs=[pl.BlockSpec(memory_space=pltpu.SMEM),
                  pl.BlockSpec((t, HEAD_DIM), lambda hh, i: (i, q_off + hh)),
                  pl.BlockSpec((s, HEAD_DIM), lambda hh, i: (0, k_off + hh)),
                  pl.BlockSpec((s, HEAD_DIM), lambda hh, i: (0, v_off + hh)),
                  pl.BlockSpec(diff_lambda.shape, lambda hh, i: (0, 0)),
                  pl.BlockSpec((1, HEAD_DIM), lambda hh, i: (0, 0))],
        out_specs=pl.BlockSpec((t, HEAD_DIM), lambda hh, i: (i, hh)),
        out_shape=jax.ShapeDtypeStruct((s, h * HEAD_DIM), BF16),
        scratch_shapes=[pltpu.VMEM((2, t, LANES), F32), pltpu.VMEM((2, t, 2 * HEAD_DIM), F32),
                        pltpu.VMEM((1, 2, t, t), F32),
                        pltpu.VMEM((2, t, t), F32), pltpu.VMEM((2, t, t), F32)],
        compiler_params=_params(2),
        name="diff_attention",
    )(t5_table.astype(F32), zb, zb, zb, diff_lambda.astype(F32), subln.reshape(1, -1).astype(F32))


def _compress_kernel(kr_ref, pos_ref, w1_ref, w2_ref, o_ref):
    half = w1_ref.shape[1] // 2
    kr = kr_ref[0, 0]
    w_top = w1_ref[0, :half, :]
    w_bot = w1_ref[0, half:, :]
    pos = pos_ref[0].astype(BF16)
    first = _dot(kr, w_top)
    second = _dot(kr, w_bot)
    pos_part = _dot(pos[:, :half], w_top) + _dot(pos[:, half:], w_bot)
    rows = kr.shape[0]
    second_next = pltpu.roll(second, rows - 1, 0)
    hid = first + second_next + pos_part[0:1, :]
    hid = hid * jax.nn.sigmoid(hid)
    o_ref[0, 0] = _dot(hid.astype(BF16), w2_ref[0]).astype(o_ref.dtype)


def compress_call(kv_rows, pos, w1, w2):
    two, g, nc, width = kv_rows.shape
    return pl.pallas_call(
        _compress_kernel,
        grid=(two, g),
        in_specs=[pl.BlockSpec((1, 1, nc, width), lambda a, b: (a, b, 0, 0)),
                  pl.BlockSpec((1, 16, 2 * width), lambda a, b: (a, 0, 0)),
                  pl.BlockSpec((1, 2 * width, HEAD_DIM), lambda a, b: (a, 0, 0)),
                  pl.BlockSpec((1, HEAD_DIM, HEAD_DIM), lambda a, b: (a, 0, 0))],
        out_specs=pl.BlockSpec((1, 1, nc, HEAD_DIM), lambda a, b: (a, b, 0, 0)),
        out_shape=jax.ShapeDtypeStruct((two, g, nc, HEAD_DIM), BF16),
        compiler_params=_params(2),
        name="nsa_compress",
    )(kv_rows, pos, w1, w2)


def _nsa_cmp_kernel(tbl_ref, q_ref, kc_ref, vc_ref, gate_ref, o_ref, sel_ref, qext_ref, *,
                    head0, qb0):
    step = pl.program_id(0)
    qb = qb0 + step
    tq = q_ref.shape[0]
    ncp = kc_ref.shape[2]
    ns = sel_ref.shape[2]
    scale = HEAD_DIM ** -0.5
    n_heads = NSA_KV_HEADS * NSA_GROUP

    @pl.when(step == 0)
    def _():
        rows = lax.broadcasted_iota(jnp.int32, (tq, LANES), 0)
        lane = lax.broadcasted_iota(jnp.int32, (tq, LANES), 1)
        slot = lane % CMP_BAND
        piece = lane // CMP_BAND
        dist = rows - (slot - CMP_BAND_BACK) * CMP_STRIDE - (CMP_LEN - 1)
        for hd in range(n_heads):
            far = tbl_ref[T5_BUCKETS - 1, head0 + hd]
            delta = (_t5_bias(tbl_ref, head0 + hd, dist) - far) * (1.0 / scale)
            hi, mid, lo = (x.astype(F32) for x in _split_bf16(delta))
            packed = jnp.where(piece == 0, hi, jnp.where(piece == 1, mid, jnp.where(piece == 2, lo, 0.0)))
            qext_ref[hd] = packed.astype(BF16)

    t_pos = qb * tq + lax.broadcasted_iota(jnp.int32, (tq, ncp), 0)
    cmp_end = lax.broadcasted_iota(jnp.int32, (tq, ncp), 1) * CMP_STRIDE + (CMP_LEN - 1)
    visible = cmp_end <= t_pos

    first_tok = qb * (tq // CMP_STRIDE) - CMP_BAND_BACK
    tok = lax.broadcasted_iota(jnp.int32, (ncp, LANES), 0)
    klane = lax.broadcasted_iota(jnp.int32, (ncp, LANES), 1)
    in_slot = (klane < 3 * CMP_BAND) & (tok == first_tok + klane % CMP_BAND)
    kext = jnp.where(in_slot, 1.0, 0.0).astype(BF16)

    ci = lax.broadcasted_iota(jnp.int32, (ncp, ns), 0) * CMP_STRIDE
    bj = lax.broadcasted_iota(jnp.int32, (ncp, ns), 1) * SLC_LEN
    overlap = jnp.where((ci <= bj + SLC_LEN - 1) & (ci + CMP_LEN - 1 >= bj), 1.0, 0.0).astype(BF16)
    t_sel = qb * tq + lax.broadcasted_iota(jnp.int32, (tq, ns), 0)
    blk = lax.broadcasted_iota(jnp.int32, (tq, ns), 1)
    cur = t_sel // SLC_LEN
    forced = (blk == 0) | (blk == cur) | (blk == cur - 1)
    valid = blk * SLC_LEN <= t_sel

    scores = []
    for grp in range(NSA_KV_HEADS):
        k_aug = jnp.concatenate([kc_ref[0, grp], kext], axis=1)
        vc = vc_ref[0, grp]
        gates = jax.nn.sigmoid(gate_ref[0, grp])
        p_sum = jnp.zeros((tq, ncp), F32)
        for r in range(NSA_GROUP):
            hd = grp * NSA_GROUP + r
            cols = slice(hd * HEAD_DIM, (hd + 1) * HEAD_DIM)
            q_aug = jnp.concatenate([q_ref[:, cols], qext_ref[hd]], axis=1)
            far2 = tbl_ref[T5_BUCKETS - 1, head0 + hd] * LOG2E
            s = jnp.where(visible, _dot_nt(q_aug, k_aug) * (scale * LOG2E), NEG_INF)
            m = jnp.max(s, -1, keepdims=True) + far2
            m = jnp.where(m == NEG_INF, 0.0, m)
            e = jnp.exp2(s - (m - far2))
            p = e * (1.0 / jnp.maximum(jnp.sum(e, -1, keepdims=True), 1e-30))
            p_sum = p_sum + p
            o = _dot(p.astype(BF16), vc) * gates[:, r:r + 1]
            o_ref[:, cols] = o.astype(o_ref.dtype)
        p_hi, p_mid, p_lo = _split_bf16(p_sum)
        imp = _dot(p_hi, overlap) + _dot(p_mid, overlap) + _dot(p_lo, overlap)
        scores.append(jnp.where(forced, SEL_BIG, jnp.where(valid, imp, -SEL_BIG)))

    score = jnp.concatenate(scores, axis=0)
    blk_f = jnp.concatenate([blk] * NSA_KV_HEADS, axis=0).astype(F32)

    def pick(_, carry):
        sc, chosen = carry
        best = jnp.max(sc, -1, keepdims=True)
        first = jnp.min(jnp.where(sc == best, blk_f, float(ns)), -1, keepdims=True)
        hit = blk_f == first
        return jnp.where(hit, NEG_INF, sc), jnp.where(hit, 1.0, chosen)

    _, chosen = lax.fori_loop(0, min(SLC_TOPN, ns), pick, (score, jnp.zeros(score.shape, F32)))
    for grp in range(NSA_KV_HEADS):
        sel_ref[grp] = chosen[grp * tq:(grp + 1) * tq].astype(sel_ref.dtype)


def nsa_cmp_call(zb, kvc, gates, t5_table, q_off, ns):
    s = zb.shape[0]
    g = NSA_KV_HEADS
    tq = CMP_QBLOCK
    ncp = kvc.shape[2]
    width = NSA_HEADS * HEAD_DIM
    n_ranges = next(r for r in (4, 2, 1) if ncp % (r * LANES) == 0)
    steps = s // tq // n_ranges
    outs = []
    for c in range(n_ranges):
        ncp_c = ncp * (c + 1) // n_ranges
        first = c * steps
        outs.append(pl.pallas_call(
            functools.partial(_nsa_cmp_kernel, head0=DIFF_HEADS, qb0=first),
            grid=(steps,),
            in_specs=[pl.BlockSpec(memory_space=pltpu.SMEM),
                      pl.BlockSpec((tq, width), lambda i, first=first: (first + i, q_off)),
                      pl.BlockSpec((1, g, ncp_c, HEAD_DIM), lambda i: (0, 0, 0, 0)),
                      pl.BlockSpec((1, g, ncp_c, HEAD_DIM), lambda i: (1, 0, 0, 0)),
                      pl.BlockSpec((1, g, tq, NSA_GROUP), lambda i, first=first: (0, 0, first + i, 0))],
            out_specs=[pl.BlockSpec((tq, width), lambda i: (i, 0)),
                       pl.BlockSpec((g, tq, ns), lambda i: (0, i, 0))],
            out_shape=[jax.ShapeDtypeStruct((steps * tq, width), BF16),
                       jax.ShapeDtypeStruct((g, steps * tq, ns), BF16)],
            scratch_shapes=[pltpu.VMEM((NSA_HEADS, tq, LANES), BF16)],
            compiler_params=_params(1),
            name="nsa_compressed",
        )(t5_table.astype(F32), zb, kvc, kvc, gates))
    o_cmp = jnp.concatenate([o for o, _ in outs], axis=0)
    sel = jnp.concatenate([m for _, m in outs], axis=1)
    return o_cmp, sel


def _nsa_sweep_kernel(tbl_ref, q_ref, k_ref, v_ref, gate_ref, *rest, t, mode, head0, branch):
    if mode == "selected":
        sel_ref, o_ref, m_ref, acc_ref, bias_ref, buf0, buf1 = rest
    else:
        o_ref, m_ref, acc_ref, bias_ref, buf0, buf1 = rest
    grp = pl.program_id(0)
    qi = pl.program_id(1)

    @pl.when(qi == 0)
    def _():
        for r in range(NSA_GROUP):
            _fill_t5_tiles(tbl_ref, head0 + grp * NSA_GROUP + r, bias_ref, r, t)

    _init_softmax_state(m_ref, acc_ref)
    scale = HEAD_DIM ** -0.5 * LOG2E
    rows, cols = _tile_rows_cols(t)

    def scores_into(ki, buf):
        k = _key_tile(k_ref, ki, t)
        for r in range(NSA_GROUP):
            buf[r] = _dot_nt(q_ref[:, r * HEAD_DIM:(r + 1) * HEAD_DIM], k)
        if mode == "selected":
            ns = sel_ref.shape[2]
            blk_of_key = (ki * t + lax.broadcasted_iota(jnp.int32, (ns, t), 1)) // SLC_LEN
            expand = jnp.where(lax.broadcasted_iota(jnp.int32, (ns, t), 0) == blk_of_key, 1.0, 0.0)
            buf[NSA_GROUP] = _dot(sel_ref[0], expand.astype(BF16))

    def consume(ki, buf, kind="far"):
        v_ones = _with_ones(_key_tile(v_ref, ki, t))
        if mode == "selected":
            keep = buf[NSA_GROUP] > 0.5
            if kind == "diag":
                keep = keep & (cols <= rows)
        else:
            keep = (cols <= rows) if kind == "diag" else (cols > rows)
        for r in range(NSA_GROUP):
            s = buf[r] * scale
            if kind == "far":
                far_bias = tbl_ref[T5_BUCKETS - 1, head0 + grp * NSA_GROUP + r] * LOG2E
                _online_softmax_step(jnp.where(keep, s, NEG_INF), far_bias, v_ones, m_ref, acc_ref, r)
                continue
            s = jnp.where(keep, s + bias_ref[r, 1 if kind == "near" else 0], NEG_INF)
            _online_softmax_step(s, 0.0, v_ones, m_ref, acc_ref, r,
                                 guard_empty_rows=(mode == "window"))

    if mode == "selected":
        _sweep_tiles_pipelined(jnp.maximum(qi - 1, 0), scores_into, consume, buf0, buf1)

    @pl.when(qi >= 1)
    def _():
        scores_into(qi - 1, buf0)
        consume(qi - 1, buf0, "near")

    scores_into(qi, buf1)
    consume(qi, buf1, "diag")

    gates = jax.nn.sigmoid(gate_ref[0, 0])
    for r in range(NSA_GROUP):
        o = _normalised(acc_ref[r]) * gates[:, r:r + 1]
        o_ref[:, r * HEAD_DIM:(r + 1) * HEAD_DIM] = o.astype(o_ref.dtype)


def nsa_sweep_call(zb, gates, t5_table, sel, q_off, k_off, v_off, t, mode):
    s = zb.shape[0]
    g = NSA_KV_HEADS
    nt = s // t
    width = NSA_GROUP * HEAD_DIM
    branch = 1 if mode == "selected" else 2
    n_buf = NSA_GROUP + 1 if mode == "selected" else NSA_GROUP
    if mode == "window":
        assert t == WINDOW
    in_specs = [pl.BlockSpec(memory_space=pltpu.SMEM),
                pl.BlockSpec((t, width), lambda gg, i: (i, q_off + gg)),
                pl.BlockSpec((s, HEAD_DIM), lambda gg, i: (0, k_off + gg)),
                pl.BlockSpec((s, HEAD_DIM), lambda gg, i: (0, v_off + gg)),
                pl.BlockSpec((1, 1, t, NSA_GROUP), lambda gg, i: (branch, gg, i, 0))]
    args = [t5_table.astype(F32), zb, zb, zb, gates]
    if mode == "selected":
        ns = sel.shape[2]
        in_specs.append(pl.BlockSpec((1, t, ns), lambda gg, i: (gg, i, 0)))
        args.append(sel)
    return pl.pallas_call(
        functools.partial(_nsa_sweep_kernel, t=t, mode=mode, head0=DIFF_HEADS, branch=branch),
        grid=(g, nt),
        in_specs=in_specs,
        out_specs=pl.BlockSpec((t, width), lambda gg, i: (i, gg)),
        out_shape=jax.ShapeDtypeStruct((s, g * width), BF16),
        scratch_shapes=[pltpu.VMEM((NSA_GROUP, t, LANES), F32),
                        pltpu.VMEM((NSA_GROUP, t, 2 * HEAD_DIM), F32),
                        pltpu.VMEM((NSA_GROUP, 2, t, t), F32),
                        pltpu.VMEM((n_buf, t, t), F32), pltpu.VMEM((n_buf, t, t), F32)],
        compiler_params=_params(2),
        name="nsa_" + mode,
    )(*args)


def _gate_columns(w_in, layer, start):
    w = w_in[layer, :, start:]
    return jnp.pad(w, ((0, 0), (0, LANES - w.shape[1])))[None]


def even_mixer(x, norm_g, w_in, layer, ret_gn, fox_fb, w_out, t):
    hb = rmsnorm_call(x, norm_g, BF16)
    main = 7 * RET_HEADS * HEAD_DIM
    zb = matmul_call(hb, w_in, layer, BF16, n=main)
    zg = matmul_call(hb, _gate_columns(w_in, layer, main), 0, F32, tn=LANES)
    ret_out = retention_call(zb, ret_gn)
    c = forget_cumsum_call(zg[:, :FOX_HEADS].T, fox_fb)
    fox_out = fox_attention_call(zb, c, 4 * RET_HEADS, 5 * RET_HEADS, 6 * RET_HEADS, t)
    return residual_matmul_call(x, [ret_out], w_out, layer, [fox_out])


def odd_mixer(x, norm_g, w_in, layer, diff_lambda, diff_subln, cmp_pos, cmp_w1, cmp_w2, w_out,
              t5_table, lambda_init, t):
    s = x.shape[0]
    hb = rmsnorm_call(x, norm_g, BF16)
    main = 5632
    zb = matmul_call(hb, w_in, layer, BF16, n=main)
    zg = matmul_call(hb, _gate_columns(w_in, layer, main), 0, F32, tn=LANES)
    hd = HEAD_DIM
    diff_out = diff_attention_call(zb, t5_table, diff_lambda, diff_subln, lambda_init, 0, 8, 16, t)
    g = NSA_KV_HEADS
    ckv = zb[:, 32 * hd:36 * hd].reshape(s // CMP_STRIDE, CMP_STRIDE, 2, g, hd)
    kv_rows = ckv.transpose(2, 3, 0, 1, 4).reshape(2, g, s // CMP_STRIDE, CMP_STRIDE * hd)
    pos = jnp.broadcast_to(cmp_pos.reshape(2, 1, CMP_LEN * hd), (2, 16, CMP_LEN * hd)).astype(F32)
    kvc = compress_call(kv_rows, pos, cmp_w1.astype(BF16), cmp_w2.astype(BF16))
    gates = zg[:, :3 * NSA_HEADS].reshape(s, 3, g, NSA_GROUP).transpose(1, 2, 0, 3)
    ns = s // SLC_LEN
    o_cmp, sel = nsa_cmp_call(zb, kvc, gates, t5_table, 3, ns)
    o_slc = nsa_sweep_call(zb, gates, t5_table, sel, 6, 36, 38, t, "selected")
    o_win = nsa_sweep_call(zb, gates, t5_table, None, 6, 40, 42, WINDOW, "window")
    return residual_matmul_call(x, [diff_out], w_out, layer, [o_cmp, o_slc, o_win])


def ffn_and_ple(x, norm_g, w_gate, w_up, w_down, p, ple_gate, ple_proj, layer):
    hb = rmsnorm_call(x, norm_g, BF16)
    u = swiglu_call(hb, w_gate, w_up, layer)
    x2, x2b = residual_matmul_call(x, [u], w_down, layer, with_bf16=True, tn=256)
    return ple_call(x2, x2b, ple_gate, p, ple_proj, layer)


def _attention_tile(s):
    return min(512, s)


def kernel(x, p, norm_mix, norm_ffn, w_in_even, ret_gn, fox_fb, w_out_even, w_in_odd, diff_lambda, diff_subln, cmp_pos, cmp_w1, cmp_w2, w_out_odd, t5_table, ffn_gate, ffn_up, ffn_down, ple_gate, ple_proj, final_norm):
    batch, s, d = x.shape
    assert batch == 1
    depth = p.shape[0]
    t = _attention_tile(s)
    xs = x[0]
    for i in range(depth):
        j = i // 2
        if i % 2 == 0:
            xs = even_mixer(xs, norm_mix[i], w_in_even, j, ret_gn[j], fox_fb[j], w_out_even, t)
        else:
            lambda_init = 0.8 - 0.6 * math.exp(-0.3 * i)
            xs = odd_mixer(xs, norm_mix[i], w_in_odd, j, diff_lambda[j], diff_subln[j], cmp_pos[j],
                           cmp_w1[j], cmp_w2[j], w_out_odd, t5_table, lambda_init, t)
        xs = ffn_and_ple(xs, norm_ffn[i], ffn_gate, ffn_up, ffn_down, p, ple_gate, ple_proj, i)
    out = rmsnorm_call(xs, final_norm, F32)
    return out[None]
```

```python
import functools
import math

import numpy as np
import jax
import jax.numpy as jnp
from jax import lax
from jax.experimental import pallas as pl
from jax.experimental.pallas import tpu as pltpu

F32 = jnp.float32
BF16 = jnp.bfloat16

HEAD_DIM = 128
NORM_EPS = 1e-6
RET_HEADS = 8
FOX_HEADS = 8
DIFF_HEADS = 8
NSA_HEADS = 8
NSA_KV_HEADS = 2
NSA_GROUP = NSA_HEADS // NSA_KV_HEADS
RET_CHUNK = 128
CMP_LEN = 32
CMP_STRIDE = 16
SLC_LEN = 64
SLC_TOPN = 16
WINDOW = 512
SEL_BIG = 1e9
T5_BUCKETS = 32
T5_MAX_DIST = 128
PLE_DIM = 256
CMP_QBLOCK = 128
CMP_BAND = 16
CMP_BAND_BACK = 9

_T5_EXACT = T5_BUCKETS // 2
T5_THRESH = tuple(
    b if b <= _T5_EXACT else int(math.ceil(
        _T5_EXACT * (T5_MAX_DIST / _T5_EXACT) ** ((b - _T5_EXACT) / (T5_BUCKETS - _T5_EXACT))))
    for b in range(T5_BUCKETS))

VMEM_LIMIT = 56 * 1024 * 1024
NEG_INF = float("-inf")
LANES = 128
SWEEP_UNROLL = 4
RET_CHUNKS_PER_STEP = 4
LOG2E = math.log2(math.e)


def _params(n_axes):
    return pltpu.CompilerParams(dimension_semantics=("arbitrary",) * n_axes,
                                vmem_limit_bytes=VMEM_LIMIT)


def _dot(a, b):
    return jnp.dot(a, b, preferred_element_type=F32)


def _dot_nt(a, b):
    return lax.dot_general(a, b, (((1,), (1,)), ((), ())), preferred_element_type=F32)


def _dot_exact(a, b):
    return jnp.dot(a, b, preferred_element_type=F32, precision=lax.Precision.HIGHEST)


def _split_bf16(x):
    hi = x.astype(BF16)
    r1 = x - hi.astype(F32)
    mid = r1.astype(BF16)
    lo = (r1 - mid.astype(F32)).astype(BF16)
    return hi, mid, lo


def _rmsnorm_kernel(x_ref, g_ref, o_ref):
    x = x_ref[...]
    y = x * lax.rsqrt(jnp.mean(x * x, -1, keepdims=True) + NORM_EPS) * g_ref[...]
    o_ref[...] = y.astype(o_ref.dtype)


def rmsnorm_call(x, g, out_dtype, tm=512):
    m, d = x.shape
    tm = min(tm, m)
    return pl.pallas_call(
        _rmsnorm_kernel,
        grid=(m // tm,),
        in_specs=[pl.BlockSpec((tm, d), lambda i: (i, 0)),
                  pl.BlockSpec((1, d), lambda i: (0, 0))],
        out_specs=pl.BlockSpec((tm, d), lambda i: (i, 0)),
        out_shape=jax.ShapeDtypeStruct((m, d), out_dtype),
        compiler_params=_params(1),
        name="rmsnorm",
    )(x, g.reshape(1, d).astype(F32))


def _weight_spec(k, tn, layer, row_block=0):
    return pl.BlockSpec((None, k, tn), lambda i, j: (layer, row_block, j))


def _normed_rows_into(x_ref, g_ref, a_ref):
    x = x_ref[...]
    y = x * lax.rsqrt(jnp.mean(x * x, -1, keepdims=True) + NORM_EPS) * g_ref[...]
    a_ref[...] = y.astype(a_ref.dtype)


def _norm_proj_kernel(x_ref, g_ref, w_ref, wgate_ref, o_ref, gate_ref, a_ref):
    @pl.when(pl.program_id(1) == 0)
    def _():
        _normed_rows_into(x_ref, g_ref, a_ref)
        gate_ref[...] = _dot(a_ref[...], wgate_ref[...])

    o_ref[...] = _dot(a_ref[...], w_ref[...]).astype(o_ref.dtype)


def norm_proj_call(x, g, w, layer, w_gate, n, tn, tm=1024):
    m, k = x.shape
    tm = min(tm, m)
    assert n % tn == 0
    return pl.pallas_call(
        _norm_proj_kernel,
        grid=(m // tm, n // tn),
        in_specs=[pl.BlockSpec((tm, k), lambda i, j: (i, 0)),
                  pl.BlockSpec((1, k), lambda i, j: (0, 0)),
                  _weight_spec(k, tn, layer),
                  pl.BlockSpec((k, LANES), lambda i, j: (0, 0))],
        out_specs=[pl.BlockSpec((tm, tn), lambda i, j: (i, j)),
                   pl.BlockSpec((tm, LANES), lambda i, j: (i, 0))],
        out_shape=[jax.ShapeDtypeStruct((m, n), BF16), jax.ShapeDtypeStruct((m, LANES), F32)],
        scratch_shapes=[pltpu.VMEM((tm, k), BF16)],
        compiler_params=_params(2),
        name="norm_proj",
    )(x, g.reshape(1, k).astype(F32), w, w_gate)


def _norm_swiglu_kernel(x_ref, g_ref, wg_ref, wu_ref, o_ref, a_ref):
    @pl.when(pl.program_id(1) == 0)
    def _():
        _normed_rows_into(x_ref, g_ref, a_ref)

    a = a_ref[...]
    gate = _dot(a, wg_ref[...])
    up = _dot(a, wu_ref[...])
    o_ref[...] = (gate * jax.nn.sigmoid(gate) * up).astype(o_ref.dtype)


def norm_swiglu_call(x, g, wg, wu, layer, tm=1024, tn=512):
    m, k = x.shape
    n = wg.shape[2]
    tm, tn = min(tm, m), min(tn, n)
    return pl.pallas_call(
        _norm_swiglu_kernel,
        grid=(m // tm, n // tn),
        in_specs=[pl.BlockSpec((tm, k), lambda i, j: (i, 0)),
                  pl.BlockSpec((1, k), lambda i, j: (0, 0)),
                  _weight_spec(k, tn, layer),
                  _weight_spec(k, tn, layer)],
        out_specs=pl.BlockSpec((tm, tn), lambda i, j: (i, j)),
        out_shape=jax.ShapeDtypeStruct((m, n), BF16),
        scratch_shapes=[pltpu.VMEM((tm, k), BF16)],
        compiler_params=_params(2),
        name="norm_swiglu",
    )(x, g.reshape(1, k).astype(F32), wg, wu)


def _mm_residual_kernel(*refs, n_a, n_b, with_bf16):
    res_ref = refs[0]
    a_refs = refs[1:1 + n_a]
    w1_ref = refs[1 + n_a]
    pos = 2 + n_a
    acc = res_ref[...]

    def summed(group):
        if len(group) == 1:
            return group[0][...]
        tot = group[0][...].astype(F32)
        for r in group[1:]:
            tot = tot + r[...].astype(F32)
        return tot.astype(BF16)

    acc = acc + _dot(summed(a_refs), w1_ref[...])
    if n_b:
        b_refs = refs[pos:pos + n_b]
        w2_ref = refs[pos + n_b]
        pos += n_b + 1
        acc = acc + _dot(summed(b_refs), w2_ref[...])
    refs[pos][...] = acc
    if with_bf16:
        refs[pos + 1][...] = acc.astype(BF16)


def residual_matmul_call(res, a_list, w, layer, b_list=(), with_bf16=False, tm=1024, tn=512):
    m, n = res.shape
    tm, tn = min(tm, m), min(tn, n)
    ka = a_list[0].shape[1]
    args = [res] + list(a_list) + [w]
    in_specs = [pl.BlockSpec((tm, tn), lambda i, j: (i, j))]
    in_specs += [pl.BlockSpec((tm, ka), lambda i, j: (i, 0)) for _ in a_list]
    in_specs += [_weight_spec(ka, tn, layer)]
    if b_list:
        kb = b_list[0].shape[1]
        assert kb == ka and w.shape[1] == ka + kb
        args += list(b_list) + [w]
        in_specs += [pl.BlockSpec((tm, kb), lambda i, j: (i, 0)) for _ in b_list]
        in_specs += [_weight_spec(kb, tn, layer, row_block=1)]
    else:
        assert w.shape[1] == ka
    out_shape = [jax.ShapeDtypeStruct((m, n), F32)]
    out_specs = [pl.BlockSpec((tm, tn), lambda i, j: (i, j))]
    if with_bf16:
        out_shape.append(jax.ShapeDtypeStruct((m, n), BF16))
        out_specs.append(pl.BlockSpec((tm, tn), lambda i, j: (i, j)))
    out = pl.pallas_call(
        functools.partial(_mm_residual_kernel, n_a=len(a_list), n_b=len(b_list), with_bf16=with_bf16),
        grid=(m // tm, n // tn),
        in_specs=in_specs,
        out_specs=out_specs,
        out_shape=out_shape,
        compiler_params=_params(2),
        name="residual_matmul",
    )(*args)
    return out if with_bf16 else out[0]


def _ple_kernel(xb_ref, xres_ref, wg_ref, p_ref, wp_ref, o_ref):
    gate = jax.nn.sigmoid(_dot(xb_ref[...], wg_ref[...]))
    emb = _dot(p_ref[...].astype(BF16), wp_ref[...])
    o_ref[...] = xres_ref[...] + gate * emb


def ple_call(x, xb, wg, p, wp, layer, tm=1024, tn=512):
    m, n = x.shape
    tm, tn = min(tm, m), min(tn, n)
    k = xb.shape[1]
    kp = p.shape[3]
    return pl.pallas_call(
        _ple_kernel,
        grid=(m // tm, n // tn),
        in_specs=[pl.BlockSpec((tm, k), lambda i, j: (i, 0)),
                  pl.BlockSpec((tm, tn), lambda i, j: (i, j)),
                  _weight_spec(k, tn, layer),
                  pl.BlockSpec((None, None, tm, kp), lambda i, j: (layer, 0, i, 0)),
                  _weight_spec(kp, tn, layer)],
        out_specs=pl.BlockSpec((tm, tn), lambda i, j: (i, j)),
        out_shape=jax.ShapeDtypeStruct((m, n), F32),
        compiler_params=_params(2),
        name="ple",
    )(xb, x, wg, p, wp)


def _retention_kernel(q_ref, k_ref, v_ref, g_ref, dec_ref, kw_ref, qw_ref, cd_ref, gn_ref,
                      o_ref, state_ref):
    @pl.when(pl.program_id(1) == 0)
    def _():
        state_ref[...] = jnp.zeros_like(state_ref)

    c = RET_CHUNK
    state = state_ref[...]
    for i in range(q_ref.shape[0] // c):
        rows = slice(i * c, (i + 1) * c)
        q = q_ref[rows, :]
        kf = k_ref[rows, :].astype(F32) * (HEAD_DIM ** -0.5)
        v = v_ref[rows, :]
        att = _dot_nt(q, kf.astype(BF16)) * dec_ref[0]
        y = _dot(att.astype(BF16), v)
        q_scaled = (q.astype(F32) * qw_ref[0]).astype(BF16)
        y = y + _dot(q_scaled, state.astype(BF16))
        k_scaled_t = (kf * kw_ref[0]).T.astype(BF16)
        state = state * cd_ref[0][0:1, :] + _dot(k_scaled_t, v)
        mu = jnp.mean(y, -1, keepdims=True)
        yc = y - mu
        var = jnp.mean(yc * yc, -1, keepdims=True)
        yn = yc * lax.rsqrt(var + NORM_EPS) * gn_ref[...]
        g = g_ref[rows, :].astype(F32)
        o_ref[rows, :] = (yn * (g * jax.nn.sigmoid(g))).astype(o_ref.dtype)
    state_ref[...] = state


def retention_call(zb, ret_gn):
    s = zb.shape[0]
    c = RET_CHUNK
    h = RET_HEADS
    log_gamma = jnp.log1p(-jnp.exp2(-5.0 - jnp.arange(h, dtype=F32)))
    pos = jnp.arange(c, dtype=F32)
    rel = pos[:, None] - pos[None, :]
    intra = jnp.where(rel >= 0, jnp.exp(log_gamma[:, None, None] * jnp.maximum(rel, 0.0)), 0.0)
    k_w = jnp.exp(log_gamma[:, None] * (c - 1 - pos))
    q_w = jnp.exp(log_gamma[:, None] * (pos + 1.0))
    chunk_decay = jnp.exp(log_gamma * c)
    kw_b = jnp.broadcast_to(k_w[:, :, None], (h, c, HEAD_DIM))
    qw_b = jnp.broadcast_to(q_w[:, :, None], (h, c, HEAD_DIM))
    cd_b = jnp.broadcast_to(chunk_decay[:, None, None], (h, 8, HEAD_DIM))

    rows = c * RET_CHUNKS_PER_STEP if s % (c * RET_CHUNKS_PER_STEP) == 0 else c

    def col(off):
        return pl.BlockSpec((rows, HEAD_DIM), lambda hh, n: (n, off + hh))

    def per_head(shape):
        return pl.BlockSpec((1,) + shape, lambda hh, n: (hh, 0, 0))

    return pl.pallas_call(
        _retention_kernel,
        grid=(h, s // rows),
        in_specs=[col(0), col(h), col(2 * h), col(3 * h),
                  per_head((c, c)), per_head((c, HEAD_DIM)), per_head((c, HEAD_DIM)),
                  per_head((8, HEAD_DIM)),
                  pl.BlockSpec((1, HEAD_DIM), lambda hh, n: (0, hh))],
        out_specs=pl.BlockSpec((rows, HEAD_DIM), lambda hh, n: (n, hh)),
        out_shape=jax.ShapeDtypeStruct((s, h * HEAD_DIM), BF16),
        scratch_shapes=[pltpu.VMEM((HEAD_DIM, HEAD_DIM), F32)],
        compiler_params=_params(2),
        name="retention",
    )(zb, zb, zb, zb, intra, kw_b, qw_b, cd_b, ret_gn.reshape(1, -1).astype(F32))


def _forget_cumsum_kernel(fb_ref, fl_ref, o_ref):
    x = fl_ref[0] + fb_ref[pl.program_id(0)]
    logf = jnp.minimum(x, 0.0) - jnp.log1p(jnp.exp(-jnp.abs(x)))
    rows = x.shape[0]
    upper = (lax.broadcasted_iota(jnp.int32, (128, 128), 0)
             <= lax.broadcasted_iota(jnp.int32, (128, 128), 1)).astype(F32)
    within = _dot_exact(logf, upper)
    totals = jnp.broadcast_to(within[:, 127:128], (rows, 128))
    strict_lower = (lax.broadcasted_iota(jnp.int32, (rows, rows), 1)
                    < lax.broadcasted_iota(jnp.int32, (rows, rows), 0)).astype(F32)
    o_ref[0] = (within + _dot_exact(strict_lower, totals)) * LOG2E


def forget_cumsum_call(fl_t, fox_fb):
    h, s = fl_t.shape
    rows = s // 128
    out = pl.pallas_call(
        _forget_cumsum_kernel,
        grid=(h,),
        in_specs=[pl.BlockSpec(memory_space=pltpu.SMEM),
                  pl.BlockSpec((1, rows, 128), lambda i: (i, 0, 0))],
        out_specs=pl.BlockSpec((1, rows, 128), lambda i: (i, 0, 0)),
        out_shape=jax.ShapeDtypeStruct((h, rows, 128), F32),
        compiler_params=_params(1),
        name="forget_cumsum",
    )(fox_fb.astype(F32), fl_t.reshape(h, rows, 128))
    return out.reshape(h, s)


def _online_softmax_step(s, offset, v_ones, m_ref, acc_ref, idx, guard_empty_rows=False):
    m_old = m_ref[idx]
    m_new = jnp.maximum(m_old, jnp.max(s, -1, keepdims=True) + offset)
    m_fin = jnp.where(m_new == NEG_INF, 0.0, m_new) if guard_empty_rows else m_new
    alpha = jnp.exp2(m_old - m_fin)
    p = jnp.exp2(s - jnp.tile(m_fin - offset, (1, s.shape[1] // LANES)))
    acc_ref[idx] = jnp.tile(alpha, (1, 2)) * acc_ref[idx] + _dot(p.astype(BF16), v_ones)
    m_ref[idx] = m_new


def _init_softmax_state(m_ref, acc_ref):
    m_ref[...] = jnp.full(m_ref.shape, NEG_INF, F32)
    acc_ref[...] = jnp.zeros(acc_ref.shape, F32)


def _with_ones(v):
    return jnp.concatenate([v, jnp.ones(v.shape, v.dtype)], axis=1)


def _normalised(acc):
    return acc[:, :HEAD_DIM] / acc[:, HEAD_DIM:]


def _tile_rows_cols(t):
    return (lax.broadcasted_iota(jnp.int32, (t, t), 0),
            lax.broadcasted_iota(jnp.int32, (t, t), 1))


def _sweep_tiles_pipelined(qi, has_near, scores_into, consume, buf0, buf1):
    bufs = (buf0, buf1)
    n_far = jnp.maximum(qi - 1, 0) if has_near else qi
    scores_into(0, buf0)

    def run(first, count):
        for u in range(count):
            scores_into(first + u + 1, bufs[(u + 1) % 2])
            consume(first + u, bufs[u % 2], "far")

    def group(j, carry):
        run(SWEEP_UNROLL * j, SWEEP_UNROLL)
        return carry

    lax.fori_loop(0, n_far // SWEEP_UNROLL, group, 0)
    rest = n_far % SWEEP_UNROLL
    step = SWEEP_UNROLL // 2
    while step >= 1:
        taken = rest - rest % (2 * step)

        @pl.when(rest % (2 * step) >= step)
        def _(first=n_far - rest + taken, count=step):
            run(first, count)

        step //= 2

    for parity in range(2):
        cur, nxt = bufs[parity], bufs[1 - parity]
        if has_near:
            @pl.when((qi >= 1) & (n_far % 2 == parity))
            def _(cur=cur, nxt=nxt):
                scores_into(qi, nxt)
                consume(qi - 1, cur, "near")
                consume(qi, nxt, "diag")
        else:
            @pl.when(n_far % 2 == parity)
            def _(cur=cur):
                consume(qi, cur, "diag")
    if has_near:
        @pl.when(qi == 0)
        def _():
            consume(qi, buf0, "diag")


def _causal_keep(t):
    return lax.broadcasted_iota(jnp.int32, (t, t), 1) <= lax.broadcasted_iota(jnp.int32, (t, t), 0)


def _key_tile(ref, ki, t):
    return ref[pl.ds(pl.multiple_of(ki * t, t), t), :]


def _fox_kernel(q_ref, k_ref, v_ref, ccol_ref, crow_ref, o_ref, m_ref, acc_ref, buf0, buf1, *, t):
    qi = pl.program_id(1)
    q = q_ref[...]
    cq = jnp.broadcast_to(ccol_ref[0], (t, LANES))
    _init_softmax_state(m_ref, acc_ref)
    scale = HEAD_DIM ** -0.5 * LOG2E

    def scores_into(ki, buf):
        buf[0] = _dot_nt(q, _key_tile(k_ref, ki, t))

    def consume(ki, buf, kind):
        c_keys = crow_ref[0, ki]

        s = buf[0] * scale - c_keys
        if kind == "diag":
            s = jnp.where(_causal_keep(t), s, NEG_INF)
        _online_softmax_step(s, cq, _with_ones(_key_tile(v_ref, ki, t)), m_ref, acc_ref, 0)

    _sweep_tiles_pipelined(qi, False, scores_into, consume, buf0, buf1)
    o_ref[...] = _normalised(acc_ref[0]).astype(o_ref.dtype)


def fox_attention_call(zb, c, q_off, k_off, v_off, t):
    s = zb.shape[0]
    h = FOX_HEADS
    nt = s // t
    c_col = c.reshape(h, s, 1)
    c_row = c.reshape(h, nt, 1, t)
    return pl.pallas_call(
        functools.partial(_fox_kernel, t=t),
        grid=(h, nt),
        in_specs=[pl.BlockSpec((t, HEAD_DIM), lambda hh, i: (i, q_off + hh)),
                  pl.BlockSpec((s, HEAD_DIM), lambda hh, i: (0, k_off + hh)),
                  pl.BlockSpec((s, HEAD_DIM), lambda hh, i: (0, v_off + hh)),
                  pl.BlockSpec((1, t, 1), lambda hh, i: (hh, i, 0)),
                  pl.BlockSpec((1, nt, 1, t), lambda hh, i: (hh, 0, 0, 0))],
        out_specs=pl.BlockSpec((t, HEAD_DIM), lambda hh, i: (i, hh)),
        out_shape=jax.ShapeDtypeStruct((s, h * HEAD_DIM), BF16),
        scratch_shapes=[pltpu.VMEM((1, t, LANES), F32), pltpu.VMEM((1, t, 2 * HEAD_DIM), F32),
                        pltpu.VMEM((1, t, t), F32), pltpu.VMEM((1, t, t), F32)],
        compiler_params=_params(2),
        name="fox_attention",
    )(zb, zb, zb, c_col, c_row)


def _t5_bias(tbl_ref, head, dist):
    bias = jnp.full(dist.shape, tbl_ref[0, head], F32)
    for b in range(1, T5_BUCKETS):
        bias = jnp.where(dist >= T5_THRESH[b], tbl_ref[b, head], bias)
    return bias


def _fill_t5_tiles(tbl_ref, head, bias_ref, slot, t):
    rows, cols = _tile_rows_cols(t)
    dist = rows - cols
    bias_ref[slot, 0] = _t5_bias(tbl_ref, head, dist) * LOG2E
    bias_ref[slot, 1] = _t5_bias(tbl_ref, head, dist + t) * LOG2E


def _diff_kernel(tbl_ref, q_ref, k_ref, v_ref, lam_ref, g_ref, o_ref,
                 m_ref, acc_ref, bias_ref, buf0, buf1, *, t, lambda_init):
    head = pl.program_id(0)
    qi = pl.program_id(1)

    @pl.when(qi == 0)
    def _():
        _fill_t5_tiles(tbl_ref, head, bias_ref, 0, t)

    q = q_ref[...]
    lane = lax.broadcasted_iota(jnp.int32, q.shape, 1)
    zero = jnp.zeros_like(q)
    q_maps = (jnp.where(lane < HEAD_DIM // 2, q, zero), jnp.where(lane >= HEAD_DIM // 2, q, zero))
    _init_softmax_state(m_ref, acc_ref)
    scale = (HEAD_DIM // 2) ** -0.5 * LOG2E
    far_bias = tbl_ref[T5_BUCKETS - 1, head] * LOG2E

    def scores_into(ki, buf):
        k = _key_tile(k_ref, ki, t)
        for mi in range(2):
            buf[mi] = _dot_nt(q_maps[mi], k)

    def consume(ki, buf, kind):
        v_ones = _with_ones(_key_tile(v_ref, ki, t))
        for mi in range(2):
            s = buf[mi] * scale
            if kind != "far":
                s = s + bias_ref[0, 1 if kind == "near" else 0]
            if kind == "diag":
                s = jnp.where(_causal_keep(t), s, NEG_INF)
            _online_softmax_step(s, far_bias if kind == "far" else 0.0, v_ones, m_ref, acc_ref, mi)

    _sweep_tiles_pipelined(qi, True, scores_into, consume, buf0, buf1)

    lam = lam_ref[...]
    lmbda = (jnp.exp(jnp.sum(lam[0:1] * lam[1:2], keepdims=True))
             - jnp.exp(jnp.sum(lam[2:3] * lam[3:4], keepdims=True)) + lambda_init)
    o = _normalised(acc_ref[0]) - lmbda * _normalised(acc_ref[1])
    y = o * lax.rsqrt(jnp.mean(o * o, -1, keepdims=True) + NORM_EPS) * g_ref[...]
    o_ref[...] = (y * (1.0 - lambda_init)).astype(o_ref.dtype)


def diff_attention_call(zb, t5_table, diff_lambda, subln, lambda_init, q_off, k_off, v_off, t):
    s = zb.shape[0]
    h = DIFF_HEADS
    nt = s // t
    return pl.pallas_call(
        functools.partial(_diff_kernel, t=t, lambda_init=lambda_init),
        grid=(h, nt),
        in_specs=[pl.BlockSpec(memory_space=pltpu.SMEM),
                  pl.BlockSpec((t, HEAD_DIM), lambda hh, i: (i, q_off + hh)),
                  pl.BlockSpec((s, HEAD_DIM), lambda hh, i: (0, k_off + hh)),
                  pl.BlockSpec((s, HEAD_DIM), lambda hh, i: (0, v_off + hh)),
                  pl.BlockSpec(diff_lambda.shape, lambda hh, i: (0, 0)),
                  pl.BlockSpec((1, HEAD_DIM), lambda hh, i: (0, 0))],
        out_specs=pl.BlockSpec((t, HEAD_DIM), lambda hh, i: (i, hh)),
        out_shape=jax.ShapeDtypeStruct((s, h * HEAD_DIM), BF16),
        scratch_shapes=[pltpu.VMEM((2, t, LANES), F32), pltpu.VMEM((2, t, 2 * HEAD_DIM), F32),
                        pltpu.VMEM((1, 2, t, t), F32),
                        pltpu.VMEM((2, t, t), F32), pltpu.VMEM((2, t, t), F32)],
        compiler_params=_params(2),
        name="diff_attention",
    )(t5_table.astype(F32), zb, zb, zb, diff_lambda.astype(F32), subln.reshape(1, -1).astype(F32))


def _compress_kernel(kr_ref, pos_ref, w1_ref, w2_ref, o_ref):
    half = w1_ref.shape[1] // 2
    kr = kr_ref[0, 0]
    w_top = w1_ref[0, :half, :]
    w_bot = w1_ref[0, half:, :]
    pos = pos_ref[0].astype(BF16)
    first = _dot(kr, w_top)
    second = _dot(kr, w_bot)
    pos_part = _dot(pos[:, :half], w_top) + _dot(pos[:, half:], w_bot)
    rows = kr.shape[0]
    second_next = pltpu.roll(second, rows - 1, 0)
    hid = first + second_next + pos_part[0:1, :]
    hid = hid * jax.nn.sigmoid(hid)
    o_ref[0, 0] = _dot(hid.astype(BF16), w2_ref[0]).astype(o_ref.dtype)


def compress_call(kv_rows, pos, w1, w2):
    two, g, nc, width = kv_rows.shape
    return pl.pallas_call(
        _compress_kernel,
        grid=(two, g),
        in_specs=[pl.BlockSpec((1, 1, nc, width), lambda a, b: (a, b, 0, 0)),
                  pl.BlockSpec((1, 16, 2 * width), lambda a, b: (a, 0, 0)),
                  pl.BlockSpec((1, 2 * width, HEAD_DIM), lambda a, b: (a, 0, 0)),
                  pl.BlockSpec((1, HEAD_DIM, HEAD_DIM), lambda a, b: (a, 0, 0))],
        out_specs=pl.BlockSpec((1, 1, nc, HEAD_DIM), lambda a, b: (a, b, 0, 0)),
        out_shape=jax.ShapeDtypeStruct((two, g, nc, HEAD_DIM), BF16),
        compiler_params=_params(2),
        name="nsa_compress",
    )(kv_rows, pos, w1, w2)


def _nsa_cmp_kernel(tbl_ref, q_ref, kc_ref, vc_ref, gate_ref, o_ref, sel_ref, qext_ref, *,
                    head0, qb0):
    step = pl.program_id(0)
    qb = qb0 + step
    tq = q_ref.shape[0]
    ncp = kc_ref.shape[2]
    ns = sel_ref.shape[2]
    scale = HEAD_DIM ** -0.5
    n_heads = NSA_KV_HEADS * NSA_GROUP

    @pl.when(step == 0)
    def _():
        rows = lax.broadcasted_iota(jnp.int32, (tq, LANES), 0)
        lane = lax.broadcasted_iota(jnp.int32, (tq, LANES), 1)
        slot = lane % CMP_BAND
        piece = lane // CMP_BAND
        dist = rows - (slot - CMP_BAND_BACK) * CMP_STRIDE - (CMP_LEN - 1)
        for hd in range(n_heads):
            far = tbl_ref[T5_BUCKETS - 1, head0 + hd]
            delta = (_t5_bias(tbl_ref, head0 + hd, dist) - far) * (1.0 / scale)
            hi, mid, lo = (x.astype(F32) for x in _split_bf16(delta))
            packed = jnp.where(piece == 0, hi, jnp.where(piece == 1, mid, jnp.where(piece == 2, lo, 0.0)))
            qext_ref[hd] = packed.astype(BF16)

    t_pos = qb * tq + lax.broadcasted_iota(jnp.int32, (tq, ncp), 0)
    cmp_end = lax.broadcasted_iota(jnp.int32, (tq, ncp), 1) * CMP_STRIDE + (CMP_LEN - 1)
    visible = cmp_end <= t_pos

    first_tok = qb * (tq // CMP_STRIDE) - CMP_BAND_BACK
    tok = lax.broadcasted_iota(jnp.int32, (ncp, LANES), 0)
    klane = lax.broadcasted_iota(jnp.int32, (ncp, LANES), 1)
    in_slot = (klane < 3 * CMP_BAND) & (tok == first_tok + klane % CMP_BAND)
    kext = jnp.where(in_slot, 1.0, 0.0).astype(BF16)

    ci = lax.broadcasted_iota(jnp.int32, (ncp, ns), 0) * CMP_STRIDE
    bj = lax.broadcasted_iota(jnp.int32, (ncp, ns), 1) * SLC_LEN
    overlap = jnp.where((ci <= bj + SLC_LEN - 1) & (ci + CMP_LEN - 1 >= bj), 1.0, 0.0).astype(BF16)
    t_sel = qb * tq + lax.broadcasted_iota(jnp.int32, (tq, ns), 0)
    blk = lax.broadcasted_iota(jnp.int32, (tq, ns), 1)
    cur = t_sel // SLC_LEN
    forced = (blk == 0) | (blk == cur) | (blk == cur - 1)
    valid = blk * SLC_LEN <= t_sel

    scores = []
    for grp in range(NSA_KV_HEADS):
        k_aug = jnp.concatenate([kc_ref[0, grp], kext], axis=1)
        vc = vc_ref[0, grp]
        gates = jax.nn.sigmoid(gate_ref[0, grp])
        p_sum = jnp.zeros((tq, ncp), F32)
        for r in range(NSA_GROUP):
            hd = grp * NSA_GROUP + r
            cols = slice(hd * HEAD_DIM, (hd + 1) * HEAD_DIM)
            q_aug = jnp.concatenate([q_ref[:, cols], qext_ref[hd]], axis=1)
            far2 = tbl_ref[T5_BUCKETS - 1, head0 + hd] * LOG2E
            s = jnp.where(visible, _dot_nt(q_aug, k_aug) * (scale * LOG2E), NEG_INF)
            m = jnp.max(s, -1, keepdims=True) + far2
            m = jnp.where(m == NEG_INF, 0.0, m)
            e = jnp.exp2(s - (m - far2))
            p = e * (1.0 / jnp.maximum(jnp.sum(e, -1, keepdims=True), 1e-30))
            p_sum = p_sum + p
            o = _dot(p.astype(BF16), vc) * gates[:, r:r + 1]
            o_ref[:, cols] = o.astype(o_ref.dtype)
        p_hi, p_mid, p_lo = _split_bf16(p_sum)
        imp = _dot(p_hi, overlap) + _dot(p_mid, overlap) + _dot(p_lo, overlap)
        scores.append(jnp.where(forced, SEL_BIG, jnp.where(valid, imp, -SEL_BIG)))

    score = jnp.concatenate(scores, axis=0)
    blk_f = jnp.concatenate([blk] * NSA_KV_HEADS, axis=0).astype(F32)

    def pick(_, carry):
        sc, chosen = carry
        best = jnp.max(sc, -1, keepdims=True)
        first = jnp.min(jnp.where(sc == best, blk_f, float(ns)), -1, keepdims=True)
        hit = blk_f == first
        return jnp.where(hit, NEG_INF, sc), jnp.where(hit, 1.0, chosen)

    _, chosen = lax.fori_loop(0, min(SLC_TOPN, ns), pick, (score, jnp.zeros(score.shape, F32)))
    for grp in range(NSA_KV_HEADS):
        sel_ref[grp] = chosen[grp * tq:(grp + 1) * tq].astype(sel_ref.dtype)


def nsa_cmp_call(zb, kvc, gates, t5_table, q_off, ns):
    s = zb.shape[0]
    g = NSA_KV_HEADS
    tq = CMP_QBLOCK
    ncp = kvc.shape[2]
    width = NSA_HEADS * HEAD_DIM
    n_ranges = next(r for r in (4, 2, 1) if ncp % (r * LANES) == 0)
    steps = s // tq // n_ranges
    outs = []
    for c in range(n_ranges):
        ncp_c = ncp * (c + 1) // n_ranges
        first = c * steps
        outs.append(pl.pallas_call(
            functools.partial(_nsa_cmp_kernel, head0=DIFF_HEADS, qb0=first),
            grid=(steps,),
            in_specs=[pl.BlockSpec(memory_space=pltpu.SMEM),
                      pl.BlockSpec((tq, width), lambda i, first=first: (first + i, q_off)),
                      pl.BlockSpec((1, g, ncp_c, HEAD_DIM), lambda i: (0, 0, 0, 0)),
                      pl.BlockSpec((1, g, ncp_c, HEAD_DIM), lambda i: (1, 0, 0, 0)),
                      pl.BlockSpec((1, g, tq, NSA_GROUP), lambda i, first=first: (0, 0, first + i, 0))],
            out_specs=[pl.BlockSpec((tq, width), lambda i: (i, 0)),
                       pl.BlockSpec((g, tq, ns), lambda i: (0, i, 0))],
            out_shape=[jax.ShapeDtypeStruct((steps * tq, width), BF16),
                       jax.ShapeDtypeStruct((g, steps * tq, ns), BF16)],
            scratch_shapes=[pltpu.VMEM((NSA_HEADS, tq, LANES), BF16)],
            compiler_params=_params(1),
            name="nsa_compressed",
        )(t5_table.astype(F32), zb, kvc, kvc, gates))
    o_cmp = jnp.concatenate([o for o, _ in outs], axis=0)
    sel = jnp.concatenate([m for _, m in outs], axis=1)
    return o_cmp, sel


def _nsa_sweep_kernel(tbl_ref, q_ref, k_ref, v_ref, gate_ref, *rest, t, mode, head0, branch):
    if mode == "selected":
        sel_ref, o_ref, m_ref, acc_ref, bias_ref, buf0, buf1 = rest
    else:
        o_ref, m_ref, acc_ref, bias_ref, buf0, buf1 = rest
    grp = pl.program_id(0)
    qi = pl.program_id(1)

    @pl.when(qi == 0)
    def _():
        for r in range(NSA_GROUP):
            _fill_t5_tiles(tbl_ref, head0 + grp * NSA_GROUP + r, bias_ref, r, t)

    _init_softmax_state(m_ref, acc_ref)
    scale = HEAD_DIM ** -0.5 * LOG2E

    def scores_into(ki, buf):
        k = _key_tile(k_ref, ki, t)
        for r in range(NSA_GROUP):
            buf[r] = _dot_nt(q_ref[:, r * HEAD_DIM:(r + 1) * HEAD_DIM], k)
        if mode == "selected":
            ns = sel_ref.shape[2]
            blk_of_key = (ki * t + lax.broadcasted_iota(jnp.int32, (ns, t), 1)) // SLC_LEN
            expand = jnp.where(lax.broadcasted_iota(jnp.int32, (ns, t), 0) == blk_of_key, 1.0, 0.0)
            buf[NSA_GROUP] = _dot(sel_ref[0], expand.astype(BF16))

    def consume(ki, buf, kind):
        v_ones = _with_ones(_key_tile(v_ref, ki, t))
        if mode == "selected":
            keep = buf[NSA_GROUP] > 0.5
            if kind == "diag":
                keep = keep & _causal_keep(t)
        elif kind == "diag":
            keep = _causal_keep(t)
        else:
            keep = jnp.logical_not(_causal_keep(t))
        for r in range(NSA_GROUP):
            s = buf[r] * scale
            if kind != "far":
                s = s + bias_ref[r, 1 if kind == "near" else 0]
            far_bias = tbl_ref[T5_BUCKETS - 1, head0 + grp * NSA_GROUP + r] * LOG2E
            _online_softmax_step(jnp.where(keep, s, NEG_INF), far_bias if kind == "far" else 0.0,
                                 v_ones, m_ref, acc_ref, r, guard_empty_rows=(mode == "window"))

    if mode == "selected":
        _sweep_tiles_pipelined(qi, True, scores_into, consume, buf0, buf1)
    else:
        @pl.when(qi >= 1)
        def _():
            scores_into(qi - 1, buf0)
            consume(qi - 1, buf0, "near")

        scores_into(qi, buf1)
        consume(qi, buf1, "diag")

    gates = jax.nn.sigmoid(gate_ref[0, 0])
    for r in range(NSA_GROUP):
        o = _normalised(acc_ref[r]) * gates[:, r:r + 1]
        o_ref[:, r * HEAD_DIM:(r + 1) * HEAD_DIM] = o.astype(o_ref.dtype)


def nsa_sweep_call(zb, gates, t5_table, sel, q_off, k_off, v_off, t, mode):
    s = zb.shape[0]
    g = NSA_KV_HEADS
    nt = s // t
    width = NSA_GROUP * HEAD_DIM
    branch = 1 if mode == "selected" else 2
    n_buf = NSA_GROUP + 1 if mode == "selected" else NSA_GROUP
    if mode == "window":
        assert t == WINDOW
    in_specs = [pl.BlockSpec(memory_space=pltpu.SMEM),
                pl.BlockSpec((t, width), lambda gg, i: (i, q_off + gg)),
                pl.BlockSpec((s, HEAD_DIM), lambda gg, i: (0, k_off + gg)),
                pl.BlockSpec((s, HEAD_DIM), lambda gg, i: (0, v_off + gg)),
                pl.BlockSpec((1, 1, t, NSA_GROUP), lambda gg, i: (branch, gg, i, 0))]
    args = [t5_table.astype(F32), zb, zb, zb, gates]
    if mode == "selected":
        ns = sel.shape[2]
        in_specs.append(pl.BlockSpec((1, t, ns), lambda gg, i: (gg, i, 0)))
        args.append(sel)
    return pl.pallas_call(
        functools.partial(_nsa_sweep_kernel, t=t, mode=mode, head0=DIFF_HEADS, branch=branch),
        grid=(g, nt),
        in_specs=in_specs,
        out_specs=pl.BlockSpec((t, width), lambda gg, i: (i, gg)),
        out_shape=jax.ShapeDtypeStruct((s, g * width), BF16),
        scratch_shapes=[pltpu.VMEM((NSA_GROUP, t, LANES), F32),
                        pltpu.VMEM((NSA_GROUP, t, 2 * HEAD_DIM), F32),
                        pltpu.VMEM((NSA_GROUP, 2, t, t), F32),
                        pltpu.VMEM((n_buf, t, t), F32), pltpu.VMEM((n_buf, t, t), F32)],
        compiler_params=_params(2),
        name="nsa_" + mode,
    )(*args)


def _gate_columns(w_in, layer, start):
    w = w_in[layer, :, start:]
    return jnp.pad(w, ((0, 0), (0, LANES - w.shape[1])))


def _column_tile(n):
    return next(tn for tn in (1024, 1408, 512, 256, 128) if n % tn == 0)


def even_mixer(x, norm_g, w_in, layer, ret_gn, fox_fb, w_out, t):
    main = 7 * RET_HEADS * HEAD_DIM
    zb, zg = norm_proj_call(x, norm_g, w_in, layer, _gate_columns(w_in, layer, main), main,
                            _column_tile(main))
    ret_out = retention_call(zb, ret_gn)
    c = forget_cumsum_call(zg[:, :FOX_HEADS].T, fox_fb)
    fox_out = fox_attention_call(zb, c, 4 * RET_HEADS, 5 * RET_HEADS, 6 * RET_HEADS, t)
    return residual_matmul_call(x, [ret_out], w_out, layer, [fox_out], tn=1024)


def odd_mixer(x, norm_g, w_in, layer, diff_lambda, diff_subln, cmp_pos, cmp_w1, cmp_w2, w_out,
              t5_table, lambda_init, t):
    s = x.shape[0]
    main = 5632
    zb, zg = norm_proj_call(x, norm_g, w_in, layer, _gate_columns(w_in, layer, main), main,
                            _column_tile(main))
    hd = HEAD_DIM
    diff_out = diff_attention_call(zb, t5_table, diff_lambda, diff_subln, lambda_init, 0, 8, 16, t)
    g = NSA_KV_HEADS
    ckv = zb[:, 32 * hd:36 * hd].reshape(s // CMP_STRIDE, CMP_STRIDE, 2, g, hd)
    kv_rows = ckv.transpose(2, 3, 0, 1, 4).reshape(2, g, s // CMP_STRIDE, CMP_STRIDE * hd)
    pos = jnp.broadcast_to(cmp_pos.reshape(2, 1, CMP_LEN * hd), (2, 16, CMP_LEN * hd)).astype(F32)
    kvc = compress_call(kv_rows, pos, cmp_w1.astype(BF16), cmp_w2.astype(BF16))
    gates = zg[:, :3 * NSA_HEADS].reshape(s, 3, g, NSA_GROUP).transpose(1, 2, 0, 3)
    ns = s // SLC_LEN
    o_cmp, sel = nsa_cmp_call(zb, kvc, gates, t5_table, 3, ns)
    o_slc = nsa_sweep_call(zb, gates, t5_table, sel, 6, 36, 38, t, "selected")
    o_win = nsa_sweep_call(zb, gates, t5_table, None, 6, 40, 42, WINDOW, "window")
    return residual_matmul_call(x, [diff_out], w_out, layer, [o_cmp, o_slc, o_win], tn=1024)


def ffn_and_ple(x, norm_g, w_gate, w_up, w_down, p, ple_gate, ple_proj, layer):
    u = norm_swiglu_call(x, norm_g, w_gate, w_up, layer)
    x2, x2b = residual_matmul_call(x, [u], w_down, layer, with_bf16=True)
    return ple_call(x2, x2b, ple_gate, p, ple_proj, layer, tn=1024)


def _attention_tile(s):
    return min(512, s)


def kernel(x, p, norm_mix, norm_ffn, w_in_even, ret_gn, fox_fb, w_out_even, w_in_odd, diff_lambda, diff_subln, cmp_pos, cmp_w1, cmp_w2, w_out_odd, t5_table, ffn_gate, ffn_up, ffn_down, ple_gate, ple_proj, final_norm):
    batch, s, d = x.shape
    assert batch == 1
    depth = p.shape[0]
    t = _attention_tile(s)
    xs = x[0]
    (w_in_even, w_out_even, w_in_odd, w_out_odd, ffn_gate, ffn_up, ffn_down, ple_gate, ple_proj) = (
        w.astype(BF16) for w in (w_in_even, w_out_even, w_in_odd, w_out_odd, ffn_gate, ffn_up,
                                 ffn_down, ple_gate, ple_proj))
    for i in range(depth):
        j = i // 2
        if i % 2 == 0:
            xs = even_mixer(xs, norm_mix[i], w_in_even, j, ret_gn[j], fox_fb[j], w_out_even, t)
        else:
            lambda_init = 0.8 - 0.6 * math.exp(-0.3 * i)
            xs = odd_mixer(xs, norm_mix[i], w_in_odd, j, diff_lambda[j], diff_subln[j], cmp_pos[j],
                           cmp_w1[j], cmp_w2[j], w_out_odd, t5_table, lambda_init, t)
        xs = ffn_and_ple(xs, norm_ffn[i], ffn_gate, ffn_up, ffn_down, p, ple_gate, ple_proj, i)
    out = rmsnorm_call(xs, final_norm, F32)
    return out[None]
```

```python
import functools
import math

import numpy as np
import jax
import jax.numpy as jnp
from jax import lax
from jax.experimental import pallas as pl
from jax.experimental.pallas import tpu as pltpu

F32 = jnp.float32
BF16 = jnp.bfloat16

HEAD_DIM = 128
NORM_EPS = 1e-6
RET_HEADS = 8
FOX_HEADS = 8
DIFF_HEADS = 8
NSA_HEADS = 8
NSA_KV_HEADS = 2
NSA_GROUP = NSA_HEADS // NSA_KV_HEADS
RET_CHUNK = 128
CMP_LEN = 32
CMP_STRIDE = 16
SLC_LEN = 64
SLC_TOPN = 16
WINDOW = 512
SEL_BIG = 1e9
T5_BUCKETS = 32
T5_MAX_DIST = 128
PLE_DIM = 256
CMP_QBLOCK = 128
CMP_BAND = 16
CMP_BAND_BACK = 9

_T5_EXACT = T5_BUCKETS // 2
T5_THRESH = tuple(
    b if b <= _T5_EXACT else int(math.ceil(
        _T5_EXACT * (T5_MAX_DIST / _T5_EXACT) ** ((b - _T5_EXACT) / (T5_BUCKETS - _T5_EXACT))))
    for b in range(T5_BUCKETS))

VMEM_LIMIT = 56 * 1024 * 1024
NEG_INF = float("-inf")
LANES = 128
SWEEP_UNROLL = 4
RET_CHUNKS_PER_STEP = 4
LOG2E = math.log2(math.e)


def _params(n_axes):
    return pltpu.CompilerParams(dimension_semantics=("arbitrary",) * n_axes,
                                vmem_limit_bytes=VMEM_LIMIT)


def _dot(a, b):
    return jnp.dot(a, b, preferred_element_type=F32)


def _dot_nt(a, b):
    return lax.dot_general(a, b, (((1,), (1,)), ((), ())), preferred_element_type=F32)


def _dot_exact(a, b):
    return jnp.dot(a, b, preferred_element_type=F32, precision=lax.Precision.HIGHEST)


def _split_bf16(x):
    hi = x.astype(BF16)
    r1 = x - hi.astype(F32)
    mid = r1.astype(BF16)
    lo = (r1 - mid.astype(F32)).astype(BF16)
    return hi, mid, lo


def _rmsnorm_kernel(x_ref, g_ref, o_ref):
    x = x_ref[...]
    y = x * lax.rsqrt(jnp.mean(x * x, -1, keepdims=True) + NORM_EPS) * g_ref[...]
    o_ref[...] = y.astype(o_ref.dtype)


def rmsnorm_call(x, g, out_dtype, tm=512):
    m, d = x.shape
    tm = min(tm, m)
    return pl.pallas_call(
        _rmsnorm_kernel,
        grid=(m // tm,),
        in_specs=[pl.BlockSpec((tm, d), lambda i: (i, 0)),
                  pl.BlockSpec((1, d), lambda i: (0, 0))],
        out_specs=pl.BlockSpec((tm, d), lambda i: (i, 0)),
        out_shape=jax.ShapeDtypeStruct((m, d), out_dtype),
        compiler_params=_params(1),
        name="rmsnorm",
    )(x, g.reshape(1, d).astype(F32))


def _weight_spec(k, tn, layer, row_block=0):
    return pl.BlockSpec((None, k, tn), lambda i, j: (layer, row_block, j))


def _normed_rows_into(x_ref, g_ref, a_ref):
    x = x_ref[...]
    y = x * lax.rsqrt(jnp.mean(x * x, -1, keepdims=True) + NORM_EPS) * g_ref[...]
    a_ref[...] = y.astype(a_ref.dtype)


def _norm_proj_kernel(x_ref, g_ref, w_ref, cs_ref, wgate_ref, o_ref, gate_ref, a_ref):
    @pl.when(pl.program_id(1) == 0)
    def _():
        _normed_rows_into(x_ref, g_ref, a_ref)
        gate_ref[...] = _dot(a_ref[...], wgate_ref[...])

    o_ref[...] = (_dot(a_ref[...], w_ref[...]) * cs_ref[...]).astype(o_ref.dtype)


def norm_proj_call(x, g, w, layer, col_scale, w_gate, n, tn, tm=1024):
    m, k = x.shape
    tm = min(tm, m)
    assert n % tn == 0
    return pl.pallas_call(
        _norm_proj_kernel,
        grid=(m // tm, n // tn),
        in_specs=[pl.BlockSpec((tm, k), lambda i, j: (i, 0)),
                  pl.BlockSpec((1, k), lambda i, j: (0, 0)),
                  _weight_spec(k, tn, layer),
                  pl.BlockSpec((1, tn), lambda i, j: (0, j)),
                  pl.BlockSpec((k, LANES), lambda i, j: (0, 0))],
        out_specs=[pl.BlockSpec((tm, tn), lambda i, j: (i, j)),
                   pl.BlockSpec((tm, LANES), lambda i, j: (i, 0))],
        out_shape=[jax.ShapeDtypeStruct((m, n), BF16), jax.ShapeDtypeStruct((m, LANES), F32)],
        scratch_shapes=[pltpu.VMEM((tm, k), BF16)],
        compiler_params=_params(2),
        name="norm_proj",
    )(x, g.reshape(1, k).astype(F32), w, col_scale, w_gate)


def _norm_swiglu_kernel(x_ref, g_ref, wg_ref, wu_ref, o_ref, a_ref):
    @pl.when(pl.program_id(1) == 0)
    def _():
        _normed_rows_into(x_ref, g_ref, a_ref)

    a = a_ref[...]
    gate = _dot(a, wg_ref[...])
    up = _dot(a, wu_ref[...])
    o_ref[...] = (gate * jax.nn.sigmoid(gate) * up).astype(o_ref.dtype)


def norm_swiglu_call(x, g, wg, wu, layer, tm=1024, tn=512):
    m, k = x.shape
    n = wg.shape[2]
    tm, tn = min(tm, m), min(tn, n)
    return pl.pallas_call(
        _norm_swiglu_kernel,
        grid=(m // tm, n // tn),
        in_specs=[pl.BlockSpec((tm, k), lambda i, j: (i, 0)),
                  pl.BlockSpec((1, k), lambda i, j: (0, 0)),
                  _weight_spec(k, tn, layer),
                  _weight_spec(k, tn, layer)],
        out_specs=pl.BlockSpec((tm, tn), lambda i, j: (i, j)),
        out_shape=jax.ShapeDtypeStruct((m, n), BF16),
        scratch_shapes=[pltpu.VMEM((tm, k), BF16)],
        compiler_params=_params(2),
        name="norm_swiglu",
    )(x, g.reshape(1, k).astype(F32), wg, wu)


def _mm_residual_kernel(*refs, n_a, n_b, with_bf16):
    res_ref = refs[0]
    a_refs = refs[1:1 + n_a]
    w1_ref = refs[1 + n_a]
    pos = 2 + n_a
    acc = res_ref[...]

    def summed(group):
        if len(group) == 1:
            return group[0][...]
        tot = group[0][...].astype(F32)
        for r in group[1:]:
            tot = tot + r[...].astype(F32)
        return tot.astype(BF16)

    acc = acc + _dot(summed(a_refs), w1_ref[...])
    if n_b:
        b_refs = refs[pos:pos + n_b]
        w2_ref = refs[pos + n_b]
        pos += n_b + 1
        acc = acc + _dot(summed(b_refs), w2_ref[...])
    refs[pos][...] = acc
    if with_bf16:
        refs[pos + 1][...] = acc.astype(BF16)


def residual_matmul_call(res, a_list, w, layer, b_list=(), with_bf16=False, tm=1024, tn=512):
    m, n = res.shape
    tm, tn = min(tm, m), min(tn, n)
    ka = a_list[0].shape[1]
    args = [res] + list(a_list) + [w]
    in_specs = [pl.BlockSpec((tm, tn), lambda i, j: (i, j))]
    in_specs += [pl.BlockSpec((tm, ka), lambda i, j: (i, 0)) for _ in a_list]
    in_specs += [_weight_spec(ka, tn, layer)]
    if b_list:
        kb = b_list[0].shape[1]
        assert kb == ka and w.shape[1] == ka + kb
        args += list(b_list) + [w]
        in_specs += [pl.BlockSpec((tm, kb), lambda i, j: (i, 0)) for _ in b_list]
        in_specs += [_weight_spec(kb, tn, layer, row_block=1)]
    else:
        assert w.shape[1] == ka
    out_shape = [jax.ShapeDtypeStruct((m, n), F32)]
    out_specs = [pl.BlockSpec((tm, tn), lambda i, j: (i, j))]
    if with_bf16:
        out_shape.append(jax.ShapeDtypeStruct((m, n), BF16))
        out_specs.append(pl.BlockSpec((tm, tn), lambda i, j: (i, j)))
    out = pl.pallas_call(
        functools.partial(_mm_residual_kernel, n_a=len(a_list), n_b=len(b_list), with_bf16=with_bf16),
        grid=(m // tm, n // tn),
        in_specs=in_specs,
        out_specs=out_specs,
        out_shape=out_shape,
        compiler_params=_params(2),
        name="residual_matmul",
    )(*args)
    return out if with_bf16 else out[0]


def _ple_kernel(xb_ref, xres_ref, wg_ref, p_ref, wp_ref, o_ref):
    gate = jax.nn.sigmoid(_dot(xb_ref[...], wg_ref[...]))
    emb = _dot(p_ref[...].astype(BF16), wp_ref[...])
    o_ref[...] = xres_ref[...] + gate * emb


def ple_call(x, xb, wg, p, wp, layer, tm=1024, tn=512):
    m, n = x.shape
    tm, tn = min(tm, m), min(tn, n)
    k = xb.shape[1]
    kp = p.shape[3]
    return pl.pallas_call(
        _ple_kernel,
        grid=(m // tm, n // tn),
        in_specs=[pl.BlockSpec((tm, k), lambda i, j: (i, 0)),
                  pl.BlockSpec((tm, tn), lambda i, j: (i, j)),
                  _weight_spec(k, tn, layer),
                  pl.BlockSpec((None, None, tm, kp), lambda i, j: (layer, 0, i, 0)),
                  _weight_spec(kp, tn, layer)],
        out_specs=pl.BlockSpec((tm, tn), lambda i, j: (i, j)),
        out_shape=jax.ShapeDtypeStruct((m, n), F32),
        compiler_params=_params(2),
        name="ple",
    )(xb, x, wg, p, wp)


def _retention_kernel(q_ref, k_ref, v_ref, g_ref, dec_ref, kw_ref, qw_ref, cd_ref, gn_ref,
                      o_ref, state_ref):
    @pl.when(pl.program_id(1) == 0)
    def _():
        state_ref[...] = jnp.zeros_like(state_ref)

    c = RET_CHUNK
    state = state_ref[...]
    for i in range(q_ref.shape[0] // c):
        rows = slice(i * c, (i + 1) * c)
        q = q_ref[rows, :]
        kf = k_ref[rows, :].astype(F32) * (HEAD_DIM ** -0.5)
        v = v_ref[rows, :]
        att = _dot_nt(q, kf.astype(BF16)) * dec_ref[0]
        y = _dot(att.astype(BF16), v)
        q_scaled = (q.astype(F32) * qw_ref[0]).astype(BF16)
        y = y + _dot(q_scaled, state.astype(BF16))
        k_scaled_t = (kf * kw_ref[0]).T.astype(BF16)
        state = state * cd_ref[0][0:1, :] + _dot(k_scaled_t, v)
        mu = jnp.mean(y, -1, keepdims=True)
        yc = y - mu
        var = jnp.mean(yc * yc, -1, keepdims=True)
        yn = yc * lax.rsqrt(var + NORM_EPS) * gn_ref[...]
        g = g_ref[rows, :].astype(F32)
        o_ref[rows, :] = (yn * (g * jax.nn.sigmoid(g))).astype(o_ref.dtype)
    state_ref[...] = state


def retention_call(zb, ret_gn):
    s = zb.shape[0]
    c = RET_CHUNK
    h = RET_HEADS
    log_gamma = jnp.log1p(-jnp.exp2(-5.0 - jnp.arange(h, dtype=F32)))
    pos = jnp.arange(c, dtype=F32)
    rel = pos[:, None] - pos[None, :]
    intra = jnp.where(rel >= 0, jnp.exp(log_gamma[:, None, None] * jnp.maximum(rel, 0.0)), 0.0)
    k_w = jnp.exp(log_gamma[:, None] * (c - 1 - pos))
    q_w = jnp.exp(log_gamma[:, None] * (pos + 1.0))
    chunk_decay = jnp.exp(log_gamma * c)
    kw_b = jnp.broadcast_to(k_w[:, :, None], (h, c, HEAD_DIM))
    qw_b = jnp.broadcast_to(q_w[:, :, None], (h, c, HEAD_DIM))
    cd_b = jnp.broadcast_to(chunk_decay[:, None, None], (h, 8, HEAD_DIM))

    rows = c * RET_CHUNKS_PER_STEP if s % (c * RET_CHUNKS_PER_STEP) == 0 else c

    def col(off):
        return pl.BlockSpec((rows, HEAD_DIM), lambda hh, n: (n, off + hh))

    def per_head(shape):
        return pl.BlockSpec((1,) + shape, lambda hh, n: (hh, 0, 0))

    return pl.pallas_call(
        _retention_kernel,
        grid=(h, s // rows),
        in_specs=[col(0), col(h), col(2 * h), col(3 * h),
                  per_head((c, c)), per_head((c, HEAD_DIM)), per_head((c, HEAD_DIM)),
                  per_head((8, HEAD_DIM)),
                  pl.BlockSpec((1, HEAD_DIM), lambda hh, n: (0, hh))],
        out_specs=pl.BlockSpec((rows, HEAD_DIM), lambda hh, n: (n, hh)),
        out_shape=jax.ShapeDtypeStruct((s, h * HEAD_DIM), BF16),
        scratch_shapes=[pltpu.VMEM((HEAD_DIM, HEAD_DIM), F32)],
        compiler_params=_params(2),
        name="retention",
    )(zb, zb, zb, zb, intra, kw_b, qw_b, cd_b, ret_gn.reshape(1, -1).astype(F32))


def _forget_cumsum_kernel(fb_ref, fl_ref, o_ref):
    x = fl_ref[0] + fb_ref[pl.program_id(0)]
    logf = jnp.minimum(x, 0.0) - jnp.log1p(jnp.exp(-jnp.abs(x)))
    rows = x.shape[0]
    upper = (lax.broadcasted_iota(jnp.int32, (128, 128), 0)
             <= lax.broadcasted_iota(jnp.int32, (128, 128), 1)).astype(F32)
    within = _dot_exact(logf, upper)
    totals = jnp.broadcast_to(within[:, 127:128], (rows, 128))
    strict_lower = (lax.broadcasted_iota(jnp.int32, (rows, rows), 1)
                    < lax.broadcasted_iota(jnp.int32, (rows, rows), 0)).astype(F32)
    o_ref[0] = (within + _dot_exact(strict_lower, totals)) * LOG2E


def forget_cumsum_call(fl_t, fox_fb):
    h, s = fl_t.shape
    rows = s // 128
    out = pl.pallas_call(
        _forget_cumsum_kernel,
        grid=(h,),
        in_specs=[pl.BlockSpec(memory_space=pltpu.SMEM),
                  pl.BlockSpec((1, rows, 128), lambda i: (i, 0, 0))],
        out_specs=pl.BlockSpec((1, rows, 128), lambda i: (i, 0, 0)),
        out_shape=jax.ShapeDtypeStruct((h, rows, 128), F32),
        compiler_params=_params(1),
        name="forget_cumsum",
    )(fox_fb.astype(F32), fl_t.reshape(h, rows, 128))
    return out.reshape(h, s)


def _online_softmax_step(s, offset, v_ones, m_ref, acc_ref, idx, guard_empty_rows=False):
    logits = s if callable(s) else (lambda which: s)
    m_old = m_ref[idx]
    m_new = jnp.maximum(m_old, jnp.max(logits(0), -1, keepdims=True) + offset)
    m_fin = jnp.where(m_new == NEG_INF, 0.0, m_new) if guard_empty_rows else m_new
    alpha = jnp.exp2(m_old - m_fin)
    s2 = logits(1)
    p = jnp.exp2(s2 - jnp.tile(m_fin - offset, (1, s2.shape[1] // LANES)))
    acc_ref[idx] = jnp.tile(alpha, (1, 2)) * acc_ref[idx] + _dot(p.astype(BF16), v_ones)
    m_ref[idx] = m_new


def _reread(buf, slot, which, zero_ref):
    if which == 0:
        return buf[slot]
    rows = buf.shape[1]
    return buf[slot, pl.ds(pl.multiple_of(zero_ref[0], rows), rows), :]


def _init_softmax_state(m_ref, acc_ref):
    m_ref[...] = jnp.full(m_ref.shape, NEG_INF, F32)
    acc_ref[...] = jnp.zeros(acc_ref.shape, F32)


def _with_ones(v):
    return jnp.concatenate([v, jnp.ones(v.shape, v.dtype)], axis=1)


def _normalised(acc):
    return acc[:, :HEAD_DIM] / acc[:, HEAD_DIM:]


def _tile_rows_cols(t):
    return (lax.broadcasted_iota(jnp.int32, (t, t), 0),
            lax.broadcasted_iota(jnp.int32, (t, t), 1))


def _sweep_tiles_pipelined(qi, has_near, scores_into, consume, buf0, buf1):
    bufs = (buf0, buf1)
    n_far = jnp.maximum(qi - 1, 0) if has_near else qi
    scores_into(0, buf0)

    def run(first, count):
        for u in range(count):
            scores_into(first + u + 1, bufs[(u + 1) % 2])
            consume(first + u, bufs[u % 2], "far")

    def group(j, carry):
        run(SWEEP_UNROLL * j, SWEEP_UNROLL)
        return carry

    lax.fori_loop(0, n_far // SWEEP_UNROLL, group, 0)
    rest = n_far % SWEEP_UNROLL
    step = SWEEP_UNROLL // 2
    while step >= 1:
        taken = rest - rest % (2 * step)

        @pl.when(rest % (2 * step) >= step)
        def _(first=n_far - rest + taken, count=step):
            run(first, count)

        step //= 2

    for parity in range(2):
        cur, nxt = bufs[parity], bufs[1 - parity]
        if has_near:
            @pl.when((qi >= 1) & (n_far % 2 == parity))
            def _(cur=cur, nxt=nxt):
                scores_into(qi, nxt)
                consume(qi - 1, cur, "near")
                consume(qi, nxt, "diag")
        else:
            @pl.when(n_far % 2 == parity)
            def _(cur=cur):
                consume(qi, cur, "diag")
    if has_near:
        @pl.when(qi == 0)
        def _():
            consume(qi, buf0, "diag")


def _causal_keep(t):
    return lax.broadcasted_iota(jnp.int32, (t, t), 1) <= lax.broadcasted_iota(jnp.int32, (t, t), 0)


def _key_tile(ref, ki, t):
    return ref[pl.ds(pl.multiple_of(ki * t, t), t), :]


def _fox_kernel(q_ref, k_ref, v_ref, ccol_ref, crow_ref, o_ref, m_ref, acc_ref, buf0, buf1, *, t):
    qi = pl.program_id(1)
    q = q_ref[...]
    cq = jnp.broadcast_to(ccol_ref[0], (t, LANES))
    _init_softmax_state(m_ref, acc_ref)

    def scores_into(ki, buf):
        buf[0] = _dot_nt(q, _key_tile(k_ref, ki, t))

    def consume(ki, buf, kind):
        c_keys = crow_ref[0, ki]
        s = buf[0] - c_keys
        if kind == "diag":
            s = jnp.where(_causal_keep(t), s, NEG_INF)
        _online_softmax_step(s, cq, _with_ones(_key_tile(v_ref, ki, t)), m_ref, acc_ref, 0)

    _sweep_tiles_pipelined(qi, False, scores_into, consume, buf0, buf1)
    o_ref[...] = _normalised(acc_ref[0]).astype(o_ref.dtype)


def fox_attention_call(zb, c, q_off, k_off, v_off, t):
    s = zb.shape[0]
    h = FOX_HEADS
    nt = s // t
    c_col = c.reshape(h, s, 1)
    c_row = c.reshape(h, nt, 1, t)
    return pl.pallas_call(
        functools.partial(_fox_kernel, t=t),
        grid=(h, nt),
        in_specs=[pl.BlockSpec((t, HEAD_DIM), lambda hh, i: (i, q_off + hh)),
                  pl.BlockSpec((s, HEAD_DIM), lambda hh, i: (0, k_off + hh)),
                  pl.BlockSpec((s, HEAD_DIM), lambda hh, i: (0, v_off + hh)),
                  pl.BlockSpec((1, t, 1), lambda hh, i: (hh, i, 0)),
                  pl.BlockSpec((1, nt, 1, t), lambda hh, i: (hh, 0, 0, 0))],
        out_specs=pl.BlockSpec((t, HEAD_DIM), lambda hh, i: (i, hh)),
        out_shape=jax.ShapeDtypeStruct((s, h * HEAD_DIM), BF16),
        scratch_shapes=[pltpu.VMEM((1, t, LANES), F32), pltpu.VMEM((1, t, 2 * HEAD_DIM), F32),
                        pltpu.VMEM((1, t, t), F32), pltpu.VMEM((1, t, t), F32)],
        compiler_params=_params(2),
        name="fox_attention",
    )(zb, zb, zb, c_col, c_row)


def _t5_bias(tbl_ref, head, dist):
    bias = jnp.full(dist.shape, tbl_ref[0, head], F32)
    for b in range(1, T5_BUCKETS):
        bias = jnp.where(dist >= T5_THRESH[b], tbl_ref[b, head], bias)
    return bias


def _fill_t5_tiles(tbl_ref, head, bias_ref, slot, t):
    rows, cols = _tile_rows_cols(t)
    dist = rows - cols
    bias_ref[slot, 0] = _t5_bias(tbl_ref, head, dist) * LOG2E
    bias_ref[slot, 1] = _t5_bias(tbl_ref, head, dist + t) * LOG2E


def _diff_kernel(tbl_ref, zero_ref, q_ref, k_ref, v_ref, lam_ref, g_ref, o_ref,
                 m_ref, acc_ref, bias_ref, buf0, buf1, *, t, lambda_init):
    head = pl.program_id(0)
    qi = pl.program_id(1)

    @pl.when(qi == 0)
    def _():
        _fill_t5_tiles(tbl_ref, head, bias_ref, 0, t)

    q = q_ref[...]
    lane = lax.broadcasted_iota(jnp.int32, q.shape, 1)
    zero = jnp.zeros_like(q)
    q_maps = (jnp.where(lane < HEAD_DIM // 2, q, zero), jnp.where(lane >= HEAD_DIM // 2, q, zero))
    _init_softmax_state(m_ref, acc_ref)
    far_bias = tbl_ref[T5_BUCKETS - 1, head] * LOG2E

    def scores_into(ki, buf):
        k = _key_tile(k_ref, ki, t)
        for mi in range(2):
            buf[mi] = _dot_nt(q_maps[mi], k)

    def consume(ki, buf, kind):
        v_ones = _with_ones(_key_tile(v_ref, ki, t))
        for mi in range(2):
            if kind == "far":
                _online_softmax_step(lambda which, mi=mi: _reread(buf, mi, which, zero_ref),
                                     far_bias, v_ones, m_ref, acc_ref, mi)
                continue
            s = buf[mi] + bias_ref[0, 1 if kind == "near" else 0]
            if kind == "diag":
                s = jnp.where(_causal_keep(t), s, NEG_INF)
            _online_softmax_step(s, 0.0, v_ones, m_ref, acc_ref, mi)

    _sweep_tiles_pipelined(qi, True, scores_into, consume, buf0, buf1)

    lam = lam_ref[...]
    lmbda = (jnp.exp(jnp.sum(lam[0:1] * lam[1:2], keepdims=True))
             - jnp.exp(jnp.sum(lam[2:3] * lam[3:4], keepdims=True)) + lambda_init)
    o = _normalised(acc_ref[0]) - lmbda * _normalised(acc_ref[1])
    y = o * lax.rsqrt(jnp.mean(o * o, -1, keepdims=True) + NORM_EPS) * g_ref[...]
    o_ref[...] = (y * (1.0 - lambda_init)).astype(o_ref.dtype)


def diff_attention_call(zb, t5_table, diff_lambda, subln, lambda_init, q_off, k_off, v_off, t):
    s = zb.shape[0]
    h = DIFF_HEADS
    nt = s // t
    return pl.pallas_call(
        functools.partial(_diff_kernel, t=t, lambda_init=lambda_init),
        grid=(h, nt),
        in_specs=[pl.BlockSpec(memory_space=pltpu.SMEM),
                  pl.BlockSpec(memory_space=pltpu.SMEM),
                  pl.BlockSpec((t, HEAD_DIM), lambda hh, i: (i, q_off + hh)),
                  pl.BlockSpec((s, HEAD_DIM), lambda hh, i: (0, k_off + hh)),
                  pl.BlockSpec((s, HEAD_DIM), lambda hh, i: (0, v_off + hh)),
                  pl.BlockSpec(diff_lambda.shape, lambda hh, i: (0, 0)),
                  pl.BlockSpec((1, HEAD_DIM), lambda hh, i: (0, 0))],
        out_specs=pl.BlockSpec((t, HEAD_DIM), lambda hh, i: (i, hh)),
        out_shape=jax.ShapeDtypeStruct((s, h * HEAD_DIM), BF16),
        scratch_shapes=[pltpu.VMEM((2, t, LANES), F32), pltpu.VMEM((2, t, 2 * HEAD_DIM), F32),
                        pltpu.VMEM((1, 2, t, t), F32),
                        pltpu.VMEM((2, t, t), F32), pltpu.VMEM((2, t, t), F32)],
        compiler_params=_params(2),
        name="diff_attention",
    )(t5_table.astype(F32), jnp.zeros((1,), jnp.int32), zb, zb, zb, diff_lambda.astype(F32),
      subln.reshape(1, -1).astype(F32))


def _compress_kernel(kr_ref, pos_ref, w1_ref, w2_ref, o_ref):
    half = w1_ref.shape[1] // 2
    kr = kr_ref[0, 0]
    w_top = w1_ref[0, :half, :]
    w_bot = w1_ref[0, half:, :]
    pos = pos_ref[0].astype(BF16)
    first = _dot(kr, w_top)
    second = _dot(kr, w_bot)
    pos_part = _dot(pos[:, :half], w_top) + _dot(pos[:, half:], w_bot)
    rows = kr.shape[0]
    second_next = pltpu.roll(second, rows - 1, 0)
    hid = first + second_next + pos_part[0:1, :]
    hid = hid * jax.nn.sigmoid(hid)
    o_ref[0, 0] = _dot(hid.astype(BF16), w2_ref[0]).astype(o_ref.dtype)


def compress_call(kv_rows, pos, w1, w2):
    two, g, nc, width = kv_rows.shape
    return pl.pallas_call(
        _compress_kernel,
        grid=(two, g),
        in_specs=[pl.BlockSpec((1, 1, nc, width), lambda a, b: (a, b, 0, 0)),
                  pl.BlockSpec((1, 16, 2 * width), lambda a, b: (a, 0, 0)),
                  pl.BlockSpec((1, 2 * width, HEAD_DIM), lambda a, b: (a, 0, 0)),
                  pl.BlockSpec((1, HEAD_DIM, HEAD_DIM), lambda a, b: (a, 0, 0))],
        out_specs=pl.BlockSpec((1, 1, nc, HEAD_DIM), lambda a, b: (a, b, 0, 0)),
        out_shape=jax.ShapeDtypeStruct((two, g, nc, HEAD_DIM), BF16),
        compiler_params=_params(2),
        name="nsa_compress",
    )(kv_rows, pos, w1, w2)


def _nsa_cmp_kernel(tbl_ref, q_ref, kc_ref, vc_ref, gate_ref, o_ref, sel_ref, qext_ref, *,
                    head0, qb0):
    step = pl.program_id(0)
    qb = qb0 + step
    tq = q_ref.shape[0]
    ncp = kc_ref.shape[2]
    ns = sel_ref.shape[2]
    n_heads = NSA_KV_HEADS * NSA_GROUP

    @pl.when(step == 0)
    def _():
        rows = lax.broadcasted_iota(jnp.int32, (tq, LANES), 0)
        lane = lax.broadcasted_iota(jnp.int32, (tq, LANES), 1)
        slot = lane % CMP_BAND
        piece = lane // CMP_BAND
        dist = rows - (slot - CMP_BAND_BACK) * CMP_STRIDE - (CMP_LEN - 1)
        for hd in range(n_heads):
            far = tbl_ref[T5_BUCKETS - 1, head0 + hd]
            delta = (_t5_bias(tbl_ref, head0 + hd, dist) - far) * LOG2E
            hi, mid, lo = (x.astype(F32) for x in _split_bf16(delta))
            packed = jnp.where(piece == 0, hi, jnp.where(piece == 1, mid, jnp.where(piece == 2, lo, 0.0)))
            qext_ref[hd] = packed.astype(BF16)

    t_pos = qb * tq + lax.broadcasted_iota(jnp.int32, (tq, ncp), 0)
    cmp_end = lax.broadcasted_iota(jnp.int32, (tq, ncp), 1) * CMP_STRIDE + (CMP_LEN - 1)
    visible = cmp_end <= t_pos

    first_tok = qb * (tq // CMP_STRIDE) - CMP_BAND_BACK
    tok = lax.broadcasted_iota(jnp.int32, (ncp, LANES), 0)
    klane = lax.broadcasted_iota(jnp.int32, (ncp, LANES), 1)
    in_slot = (klane < 3 * CMP_BAND) & (tok == first_tok + klane % CMP_BAND)
    kext = jnp.where(in_slot, 1.0, 0.0).astype(BF16)

    ci = lax.broadcasted_iota(jnp.int32, (ncp, ns), 0) * CMP_STRIDE
    bj = lax.broadcasted_iota(jnp.int32, (ncp, ns), 1) * SLC_LEN
    overlap = jnp.where((ci <= bj + SLC_LEN - 1) & (ci + CMP_LEN - 1 >= bj), 1.0, 0.0).astype(BF16)
    t_sel = qb * tq + lax.broadcasted_iota(jnp.int32, (tq, ns), 0)
    blk = lax.broadcasted_iota(jnp.int32, (tq, ns), 1)
    cur = t_sel // SLC_LEN
    forced = (blk == 0) | (blk == cur) | (blk == cur - 1)
    valid = blk * SLC_LEN <= t_sel

    scores = []
    for grp in range(NSA_KV_HEADS):
        k_aug = jnp.concatenate([kc_ref[0, grp], kext], axis=1)
        vc = vc_ref[0, grp]
        gates = jax.nn.sigmoid(gate_ref[0, grp])
        p_sum = jnp.zeros((tq, ncp), F32)
        for r in range(NSA_GROUP):
            hd = grp * NSA_GROUP + r
            cols = slice(hd * HEAD_DIM, (hd + 1) * HEAD_DIM)
            q_aug = jnp.concatenate([q_ref[:, cols], qext_ref[hd]], axis=1)
            far2 = tbl_ref[T5_BUCKETS - 1, head0 + hd] * LOG2E
            s = jnp.where(visible, _dot_nt(q_aug, k_aug), NEG_INF)
            m = jnp.max(s, -1, keepdims=True) + far2
            m = jnp.where(m == NEG_INF, 0.0, m)
            e = jnp.exp2(s - (m - far2))
            p = e * (1.0 / jnp.maximum(jnp.sum(e, -1, keepdims=True), 1e-30))
            p_sum = p_sum + p
            o = _dot(p.astype(BF16), vc) * gates[:, r:r + 1]
            o_ref[:, cols] = o.astype(o_ref.dtype)
        p_hi, p_mid, p_lo = _split_bf16(p_sum)
        imp = _dot(p_hi, overlap) + _dot(p_mid, overlap) + _dot(p_lo, overlap)
        scores.append(jnp.where(forced, SEL_BIG, jnp.where(valid, imp, -SEL_BIG)))

    score = jnp.concatenate(scores, axis=0)
    blk_f = jnp.concatenate([blk] * NSA_KV_HEADS, axis=0).astype(F32)

    def pick(_, carry):
        sc, chosen = carry
        best = jnp.max(sc, -1, keepdims=True)
        first = jnp.min(jnp.where(sc == best, blk_f, float(ns)), -1, keepdims=True)
        hit = blk_f == first
        return jnp.where(hit, NEG_INF, sc), jnp.where(hit, 1.0, chosen)

    _, chosen = lax.fori_loop(0, min(SLC_TOPN, ns), pick, (score, jnp.zeros(score.shape, F32)))
    for grp in range(NSA_KV_HEADS):
        sel_ref[grp] = chosen[grp * tq:(grp + 1) * tq].astype(sel_ref.dtype)


def nsa_cmp_call(zb, kvc, gates, t5_table, q_off, ns):
    s = zb.shape[0]
    g = NSA_KV_HEADS
    tq = CMP_QBLOCK
    ncp = kvc.shape[2]
    width = NSA_HEADS * HEAD_DIM
    n_ranges = next(r for r in (4, 2, 1) if ncp % (r * LANES) == 0)
    steps = s // tq // n_ranges
    outs = []
    for c in range(n_ranges):
        ncp_c = ncp * (c + 1) // n_ranges
        first = c * steps
        outs.append(pl.pallas_call(
            functools.partial(_nsa_cmp_kernel, head0=DIFF_HEADS, qb0=first),
            grid=(steps,),
            in_specs=[pl.BlockSpec(memory_space=pltpu.SMEM),
                      pl.BlockSpec((tq, width), lambda i, first=first: (first + i, q_off)),
                      pl.BlockSpec((1, g, ncp_c, HEAD_DIM), lambda i: (0, 0, 0, 0)),
                      pl.BlockSpec((1, g, ncp_c, HEAD_DIM), lambda i: (1, 0, 0, 0)),
                      pl.BlockSpec((1, g, tq, NSA_GROUP), lambda i, first=first: (0, 0, first + i, 0))],
            out_specs=[pl.BlockSpec((tq, width), lambda i: (i, 0)),
                       pl.BlockSpec((g, tq, ns), lambda i: (0, i, 0))],
            out_shape=[jax.ShapeDtypeStruct((steps * tq, width), BF16),
                       jax.ShapeDtypeStruct((g, steps * tq, ns), BF16)],
            scratch_shapes=[pltpu.VMEM((NSA_HEADS, tq, LANES), BF16)],
            compiler_params=_params(1),
            name="nsa_compressed",
        )(t5_table.astype(F32), zb, kvc, kvc, gates))
    o_cmp = jnp.concatenate([o for o, _ in outs], axis=0)
    sel = jnp.concatenate([m for _, m in outs], axis=1)
    return o_cmp, sel


def _nsa_sweep_kernel(tbl_ref, q_ref, k_ref, v_ref, gate_ref, *rest, t, mode, head0, branch):
    if mode == "selected":
        sel_ref, o_ref, m_ref, acc_ref, bias_ref, buf0, buf1 = rest
    else:
        o_ref, m_ref, acc_ref, bias_ref, buf0, buf1 = rest
    grp = pl.program_id(0)
    qi = pl.program_id(1)

    @pl.when(qi == 0)
    def _():
        for r in range(NSA_GROUP):
            _fill_t5_tiles(tbl_ref, head0 + grp * NSA_GROUP + r, bias_ref, r, t)

    _init_softmax_state(m_ref, acc_ref)

    def scores_into(ki, buf):
        k = _key_tile(k_ref, ki, t)
        for r in range(NSA_GROUP):
            buf[r] = _dot_nt(q_ref[:, r * HEAD_DIM:(r + 1) * HEAD_DIM], k)
        if mode == "selected":
            ns = sel_ref.shape[2]
            blk_of_key = (ki * t + lax.broadcasted_iota(jnp.int32, (ns, t), 1)) // SLC_LEN
            expand = jnp.where(lax.broadcasted_iota(jnp.int32, (ns, t), 0) == blk_of_key, 1.0, 0.0)
            buf[NSA_GROUP] = _dot(sel_ref[0], expand.astype(BF16))

    def consume(ki, buf, kind):
        v_ones = _with_ones(_key_tile(v_ref, ki, t))
        if mode == "selected":
            keep = buf[NSA_GROUP] > 0.5
            if kind == "diag":
                keep = keep & _causal_keep(t)
        elif kind == "diag":
            keep = _causal_keep(t)
        else:
            keep = jnp.logical_not(_causal_keep(t))
        for r in range(NSA_GROUP):
            s = buf[r]
            if kind != "far":
                s = s + bias_ref[r, 1 if kind == "near" else 0]
            far_bias = tbl_ref[T5_BUCKETS - 1, head0 + grp * NSA_GROUP + r] * LOG2E
            _online_softmax_step(jnp.where(keep, s, NEG_INF), far_bias if kind == "far" else 0.0,
                                 v_ones, m_ref, acc_ref, r, guard_empty_rows=(mode == "window"))

    if mode == "selected":
        _sweep_tiles_pipelined(qi, True, scores_into, consume, buf0, buf1)
    else:
        @pl.when(qi >= 1)
        def _():
            scores_into(qi - 1, buf0)
            consume(qi - 1, buf0, "near")

        scores_into(qi, buf1)
        consume(qi, buf1, "diag")

    gates = jax.nn.sigmoid(gate_ref[0, 0])
    for r in range(NSA_GROUP):
        o = _normalised(acc_ref[r]) * gates[:, r:r + 1]
        o_ref[:, r * HEAD_DIM:(r + 1) * HEAD_DIM] = o.astype(o_ref.dtype)


def nsa_sweep_call(zb, gates, t5_table, sel, q_off, k_off, v_off, t, mode):
    s = zb.shape[0]
    g = NSA_KV_HEADS
    nt = s // t
    width = NSA_GROUP * HEAD_DIM
    branch = 1 if mode == "selected" else 2
    n_buf = NSA_GROUP + 1 if mode == "selected" else NSA_GROUP
    if mode == "window":
        assert t == WINDOW
    in_specs = [pl.BlockSpec(memory_space=pltpu.SMEM),
                pl.BlockSpec((t, width), lambda gg, i: (i, q_off + gg)),
                pl.BlockSpec((s, HEAD_DIM), lambda gg, i: (0, k_off + gg)),
                pl.BlockSpec((s, HEAD_DIM), lambda gg, i: (0, v_off + gg)),
                pl.BlockSpec((1, 1, t, NSA_GROUP), lambda gg, i: (branch, gg, i, 0))]
    args = [t5_table.astype(F32), zb, zb, zb, gates]
    if mode == "selected":
        ns = sel.shape[2]
        in_specs.append(pl.BlockSpec((1, t, ns), lambda gg, i: (gg, i, 0)))
        args.append(sel)
    return pl.pallas_call(
        functools.partial(_nsa_sweep_kernel, t=t, mode=mode, head0=DIFF_HEADS, branch=branch),
        grid=(g, nt),
        in_specs=in_specs,
        out_specs=pl.BlockSpec((t, width), lambda gg, i: (i, gg)),
        out_shape=jax.ShapeDtypeStruct((s, g * width), BF16),
        scratch_shapes=[pltpu.VMEM((NSA_GROUP, t, LANES), F32),
                        pltpu.VMEM((NSA_GROUP, t, 2 * HEAD_DIM), F32),
                        pltpu.VMEM((NSA_GROUP, 2, t, t), F32),
                        pltpu.VMEM((n_buf, t, t), F32), pltpu.VMEM((n_buf, t, t), F32)],
        compiler_params=_params(2),
        name="nsa_" + mode,
    )(*args)


def _gate_columns(w_in, layer, start):
    w = w_in[layer, :, start:]
    return jnp.pad(w, ((0, 0), (0, LANES - w.shape[1])))


def _column_tile(n):
    return next(tn for tn in (1024, 1408, 512, 256, 128) if n % tn == 0)


def _query_column_scale(n, spans):
    cs = np.ones((1, n), np.float32)
    for start, stop, width in spans:
        cs[0, start:stop] = width ** -0.5 * LOG2E
    return jnp.asarray(cs)


def even_mixer(x, norm_g, w_in, layer, ret_gn, fox_fb, w_out, t):
    main = 7 * RET_HEADS * HEAD_DIM
    fox_q = 4 * RET_HEADS * HEAD_DIM
    col_scale = _query_column_scale(main, [(fox_q, fox_q + FOX_HEADS * HEAD_DIM, HEAD_DIM)])
    zb, zg = norm_proj_call(x, norm_g, w_in, layer, col_scale, _gate_columns(w_in, layer, main),
                            main, _column_tile(main))
    ret_out = retention_call(zb, ret_gn)
    c = forget_cumsum_call(zg[:, :FOX_HEADS].T, fox_fb)
    fox_out = fox_attention_call(zb, c, 4 * RET_HEADS, 5 * RET_HEADS, 6 * RET_HEADS, t)
    return residual_matmul_call(x, [ret_out], w_out, layer, [fox_out], tn=1024)


def odd_mixer(x, norm_g, w_in, layer, diff_lambda, diff_subln, cmp_pos, cmp_w1, cmp_w2, w_out,
              t5_table, lambda_init, t):
    s = x.shape[0]
    main = 5632
    hd = HEAD_DIM
    nsa_q = 3 * DIFF_HEADS * hd
    col_scale = _query_column_scale(main, [(0, DIFF_HEADS * hd, hd // 2),
                                           (nsa_q, nsa_q + NSA_HEADS * hd, hd)])
    zb, zg = norm_proj_call(x, norm_g, w_in, layer, col_scale, _gate_columns(w_in, layer, main),
                            main, _column_tile(main))
    diff_out = diff_attention_call(zb, t5_table, diff_lambda, diff_subln, lambda_init, 0, 8, 16, t)
    g = NSA_KV_HEADS
    ckv = zb[:, 32 * hd:36 * hd].reshape(s // CMP_STRIDE, CMP_STRIDE, 2, g, hd)
    kv_rows = ckv.transpose(2, 3, 0, 1, 4).reshape(2, g, s // CMP_STRIDE, CMP_STRIDE * hd)
    pos = jnp.broadcast_to(cmp_pos.reshape(2, 1, CMP_LEN * hd), (2, 16, CMP_LEN * hd)).astype(F32)
    kvc = compress_call(kv_rows, pos, cmp_w1.astype(BF16), cmp_w2.astype(BF16))
    gates = zg[:, :3 * NSA_HEADS].reshape(s, 3, g, NSA_GROUP).transpose(1, 2, 0, 3)
    ns = s // SLC_LEN
    o_cmp, sel = nsa_cmp_call(zb, kvc, gates, t5_table, 3, ns)
    o_slc = nsa_sweep_call(zb, gates, t5_table, sel, 6, 36, 38, t, "selected")
    o_win = nsa_sweep_call(zb, gates, t5_table, None, 6, 40, 42, WINDOW, "window")
    return residual_matmul_call(x, [diff_out], w_out, layer, [o_cmp, o_slc, o_win], tn=1024)


def ffn_and_ple(x, norm_g, w_gate, w_up, w_down, p, ple_gate, ple_proj, layer):
    u = norm_swiglu_call(x, norm_g, w_gate, w_up, layer)
    x2, x2b = residual_matmul_call(x, [u], w_down, layer, with_bf16=True)
    return ple_call(x2, x2b, ple_gate, p, ple_proj, layer, tn=1024)


def _attention_tile(s):
    return min(512, s)


def kernel(x, p, norm_mix, norm_ffn, w_in_even, ret_gn, fox_fb, w_out_even, w_in_odd, diff_lambda, diff_subln, cmp_pos, cmp_w1, cmp_w2, w_out_odd, t5_table, ffn_gate, ffn_up, ffn_down, ple_gate, ple_proj, final_norm):
    batch, s, d = x.shape
    assert batch == 1
    depth = p.shape[0]
    t = _attention_tile(s)
    xs = x[0]
    (w_in_even, w_out_even, w_in_odd, w_out_odd, ffn_gate, ffn_up, ffn_down, ple_gate, ple_proj) = (
        w.astype(BF16) for w in (w_in_even, w_out_even, w_in_odd, w_out_odd, ffn_gate, ffn_up,
                                 ffn_down, ple_gate, ple_proj))
    for i in range(depth):
        j = i // 2
        if i % 2 == 0:
            xs = even_mixer(xs, norm_mix[i], w_in_even, j, ret_gn[j], fox_fb[j], w_out_even, t)
        else:
            lambda_init = 0.8 - 0.6 * math.exp(-0.3 * i)
            xs = odd_mixer(xs, norm_mix[i], w_in_odd, j, diff_lambda[j], diff_subln[j], cmp_pos[j],
                           cmp_w1[j], cmp_w2[j], w_out_odd, t5_table, lambda_init, t)
        xs = ffn_and_ple(xs, norm_ffn[i], ffn_gate, ffn_up, ffn_down, p, ple_gate, ple_proj, i)
    out = rmsnorm_call(xs, final_norm, F32)
    return out[None]
```

```python
import functools
import math

import numpy as np
import jax
import jax.numpy as jnp
from jax import lax
from jax.experimental import pallas as pl
from jax.experimental.pallas import tpu as pltpu

F32 = jnp.float32
BF16 = jnp.bfloat16

HEAD_DIM = 128
NORM_EPS = 1e-6
RET_HEADS = 8
FOX_HEADS = 8
DIFF_HEADS = 8
NSA_HEADS = 8
NSA_KV_HEADS = 2
NSA_GROUP = NSA_HEADS // NSA_KV_HEADS
RET_CHUNK = 128
CMP_LEN = 32
CMP_STRIDE = 16
SLC_LEN = 64
SLC_TOPN = 16
WINDOW = 512
SEL_BIG = 1e9
T5_BUCKETS = 32
T5_MAX_DIST = 128
PLE_DIM = 256
CMP_QBLOCK = 128
CMP_BAND = 16
CMP_BAND_BACK = 9

_T5_EXACT = T5_BUCKETS // 2
T5_THRESH = tuple(
    b if b <= _T5_EXACT else int(math.ceil(
        _T5_EXACT * (T5_MAX_DIST / _T5_EXACT) ** ((b - _T5_EXACT) / (T5_BUCKETS - _T5_EXACT))))
    for b in range(T5_BUCKETS))

VMEM_LIMIT = 56 * 1024 * 1024
NEG_INF = float("-inf")
LANES = 128
SWEEP_UNROLL = 8
SWEEP_UNROLL_4HEADS = 4
RET_CHUNKS_PER_STEP = 4
LOG2E = math.log2(math.e)


def _params(n_axes):
    return pltpu.CompilerParams(dimension_semantics=("arbitrary",) * n_axes,
                                vmem_limit_bytes=VMEM_LIMIT)


def _dot(a, b):
    return jnp.dot(a, b, preferred_element_type=F32)


def _dot_nt(a, b):
    return lax.dot_general(a, b, (((1,), (1,)), ((), ())), preferred_element_type=F32)


def _dot_exact(a, b):
    return jnp.dot(a, b, preferred_element_type=F32, precision=lax.Precision.HIGHEST)


def _split_bf16(x):
    hi = x.astype(BF16)
    r1 = x - hi.astype(F32)
    mid = r1.astype(BF16)
    lo = (r1 - mid.astype(F32)).astype(BF16)
    return hi, mid, lo


def _rmsnorm_kernel(x_ref, g_ref, o_ref):
    x = x_ref[...]
    y = x * lax.rsqrt(jnp.mean(x * x, -1, keepdims=True) + NORM_EPS) * g_ref[...]
    o_ref[...] = y.astype(o_ref.dtype)


def rmsnorm_call(x, g, out_dtype, tm=512):
    m, d = x.shape
    tm = min(tm, m)
    return pl.pallas_call(
        _rmsnorm_kernel,
        grid=(m // tm,),
        in_specs=[pl.BlockSpec((tm, d), lambda i: (i, 0)),
                  pl.BlockSpec((1, d), lambda i: (0, 0))],
        out_specs=pl.BlockSpec((tm, d), lambda i: (i, 0)),
        out_shape=jax.ShapeDtypeStruct((m, d), out_dtype),
        compiler_params=_params(1),
        name="rmsnorm",
    )(x, g.reshape(1, d).astype(F32))


def _weight_spec(k, tn, layer, row_block=0):
    return pl.BlockSpec((None, k, tn), lambda i, j: (layer, row_block, j))


def _normed_rows_into(x_ref, g_ref, a_ref):
    x = x_ref[...]
    y = x * lax.rsqrt(jnp.mean(x * x, -1, keepdims=True) + NORM_EPS) * g_ref[...]
    a_ref[...] = y.astype(a_ref.dtype)


def _norm_proj_kernel(x_ref, g_ref, w_ref, cs_ref, wgate_ref, o_ref, gate_ref, a_ref):
    @pl.when(pl.program_id(1) == 0)
    def _():
        _normed_rows_into(x_ref, g_ref, a_ref)
        gate_ref[...] = _dot(a_ref[...], wgate_ref[...])

    o_ref[...] = (_dot(a_ref[...], w_ref[...]) * cs_ref[...]).astype(o_ref.dtype)


def norm_proj_call(x, g, w, layer, col_scale, w_gate, n, tn, tm=1024):
    m, k = x.shape
    tm = min(tm, m)
    assert n % tn == 0
    return pl.pallas_call(
        _norm_proj_kernel,
        grid=(m // tm, n // tn),
        in_specs=[pl.BlockSpec((tm, k), lambda i, j: (i, 0)),
                  pl.BlockSpec((1, k), lambda i, j: (0, 0)),
                  _weight_spec(k, tn, layer),
                  pl.BlockSpec((1, tn), lambda i, j: (0, j)),
                  pl.BlockSpec((k, LANES), lambda i, j: (0, 0))],
        out_specs=[pl.BlockSpec((tm, tn), lambda i, j: (i, j)),
                   pl.BlockSpec((tm, LANES), lambda i, j: (i, 0))],
        out_shape=[jax.ShapeDtypeStruct((m, n), BF16), jax.ShapeDtypeStruct((m, LANES), F32)],
        scratch_shapes=[pltpu.VMEM((tm, k), BF16)],
        compiler_params=_params(2),
        name="norm_proj",
    )(x, g.reshape(1, k).astype(F32), w, col_scale, w_gate)


def _norm_swiglu_kernel(x_ref, g_ref, wg_ref, wu_ref, o_ref, a_ref):
    @pl.when(pl.program_id(1) == 0)
    def _():
        _normed_rows_into(x_ref, g_ref, a_ref)

    a = a_ref[...]
    gate = _dot(a, wg_ref[...])
    up = _dot(a, wu_ref[...])
    o_ref[...] = (gate * jax.nn.sigmoid(gate) * up).astype(o_ref.dtype)


def norm_swiglu_call(x, g, wg, wu, layer, tm=1024, tn=512):
    m, k = x.shape
    n = wg.shape[2]
    tm, tn = min(tm, m), min(tn, n)
    return pl.pallas_call(
        _norm_swiglu_kernel,
        grid=(m // tm, n // tn),
        in_specs=[pl.BlockSpec((tm, k), lambda i, j: (i, 0)),
                  pl.BlockSpec((1, k), lambda i, j: (0, 0)),
                  _weight_spec(k, tn, layer),
                  _weight_spec(k, tn, layer)],
        out_specs=pl.BlockSpec((tm, tn), lambda i, j: (i, j)),
        out_shape=jax.ShapeDtypeStruct((m, n), BF16),
        scratch_shapes=[pltpu.VMEM((tm, k), BF16)],
        compiler_params=_params(2),
        name="norm_swiglu",
    )(x, g.reshape(1, k).astype(F32), wg, wu)


def _mm_residual_kernel(*refs, n_a, n_b, with_bf16):
    res_ref = refs[0]
    a_refs = refs[1:1 + n_a]
    w1_ref = refs[1 + n_a]
    pos = 2 + n_a
    acc = res_ref[...]

    def summed(group):
        if len(group) == 1:
            return group[0][...]
        tot = group[0][...].astype(F32)
        for r in group[1:]:
            tot = tot + r[...].astype(F32)
        return tot.astype(BF16)

    acc = acc + _dot(summed(a_refs), w1_ref[...])
    if n_b:
        b_refs = refs[pos:pos + n_b]
        w2_ref = refs[pos + n_b]
        pos += n_b + 1
        acc = acc + _dot(summed(b_refs), w2_ref[...])
    refs[pos][...] = acc
    if with_bf16:
        refs[pos + 1][...] = acc.astype(BF16)


def residual_matmul_call(res, a_list, w, layer, b_list=(), with_bf16=False, tm=1024, tn=512):
    m, n = res.shape
    tm, tn = min(tm, m), min(tn, n)
    ka = a_list[0].shape[1]
    args = [res] + list(a_list) + [w]
    in_specs = [pl.BlockSpec((tm, tn), lambda i, j: (i, j))]
    in_specs += [pl.BlockSpec((tm, ka), lambda i, j: (i, 0)) for _ in a_list]
    in_specs += [_weight_spec(ka, tn, layer)]
    if b_list:
        kb = b_list[0].shape[1]
        assert kb == ka and w.shape[1] == ka + kb
        args += list(b_list) + [w]
        in_specs += [pl.BlockSpec((tm, kb), lambda i, j: (i, 0)) for _ in b_list]
        in_specs += [_weight_spec(kb, tn, layer, row_block=1)]
    else:
        assert w.shape[1] == ka
    out_shape = [jax.ShapeDtypeStruct((m, n), F32)]
    out_specs = [pl.BlockSpec((tm, tn), lambda i, j: (i, j))]
    if with_bf16:
        out_shape.append(jax.ShapeDtypeStruct((m, n), BF16))
        out_specs.append(pl.BlockSpec((tm, tn), lambda i, j: (i, j)))
    out = pl.pallas_call(
        functools.partial(_mm_residual_kernel, n_a=len(a_list), n_b=len(b_list), with_bf16=with_bf16),
        grid=(m // tm, n // tn),
        in_specs=in_specs,
        out_specs=out_specs,
        out_shape=out_shape,
        compiler_params=_params(2),
        name="residual_matmul",
    )(*args)
    return out if with_bf16 else out[0]


def _ple_kernel(xb_ref, xres_ref, wg_ref, p_ref, wp_ref, o_ref):
    gate = jax.nn.sigmoid(_dot(xb_ref[...], wg_ref[...]))
    emb = _dot(p_ref[...].astype(BF16), wp_ref[...])
    o_ref[...] = xres_ref[...] + gate * emb


def ple_call(x, xb, wg, p, wp, layer, tm=1024, tn=512):
    m, n = x.shape
    tm, tn = min(tm, m), min(tn, n)
    k = xb.shape[1]
    kp = p.shape[3]
    return pl.pallas_call(
        _ple_kernel,
        grid=(m // tm, n // tn),
        in_specs=[pl.BlockSpec((tm, k), lambda i, j: (i, 0)),
                  pl.BlockSpec((tm, tn), lambda i, j: (i, j)),
                  _weight_spec(k, tn, layer),
                  pl.BlockSpec((None, None, tm, kp), lambda i, j: (layer, 0, i, 0)),
                  _weight_spec(kp, tn, layer)],
        out_specs=pl.BlockSpec((tm, tn), lambda i, j: (i, j)),
        out_shape=jax.ShapeDtypeStruct((m, n), F32),
        compiler_params=_params(2),
        name="ple",
    )(xb, x, wg, p, wp)


def _retention_kernel(q_ref, k_ref, v_ref, g_ref, dec_ref, kw_ref, qw_ref, cd_ref, gn_ref,
                      o_ref, state_ref):
    @pl.when(pl.program_id(1) == 0)
    def _():
        state_ref[...] = jnp.zeros_like(state_ref)

    c = RET_CHUNK
    state = state_ref[...]
    for i in range(q_ref.shape[0] // c):
        rows = slice(i * c, (i + 1) * c)
        q = q_ref[rows, :]
        kf = k_ref[rows, :].astype(F32) * (HEAD_DIM ** -0.5)
        v = v_ref[rows, :]
        att = _dot_nt(q, kf.astype(BF16)) * dec_ref[0]
        y = _dot(att.astype(BF16), v)
        q_scaled = (q.astype(F32) * qw_ref[0]).astype(BF16)
        y = y + _dot(q_scaled, state.astype(BF16))
        k_scaled_t = (kf * kw_ref[0]).T.astype(BF16)
        state = state * cd_ref[0][0:1, :] + _dot(k_scaled_t, v)
        mu = jnp.mean(y, -1, keepdims=True)
        yc = y - mu
        var = jnp.mean(yc * yc, -1, keepdims=True)
        yn = yc * lax.rsqrt(var + NORM_EPS) * gn_ref[...]
        g = g_ref[rows, :].astype(F32)
        o_ref[rows, :] = (yn * (g * jax.nn.sigmoid(g))).astype(o_ref.dtype)
    state_ref[...] = state


def retention_call(zb, ret_gn):
    s = zb.shape[0]
    c = RET_CHUNK
    h = RET_HEADS
    log_gamma = jnp.log1p(-jnp.exp2(-5.0 - jnp.arange(h, dtype=F32)))
    pos = jnp.arange(c, dtype=F32)
    rel = pos[:, None] - pos[None, :]
    intra = jnp.where(rel >= 0, jnp.exp(log_gamma[:, None, None] * jnp.maximum(rel, 0.0)), 0.0)
    k_w = jnp.exp(log_gamma[:, None] * (c - 1 - pos))
    q_w = jnp.exp(log_gamma[:, None] * (pos + 1.0))
    chunk_decay = jnp.exp(log_gamma * c)
    kw_b = jnp.broadcast_to(k_w[:, :, None], (h, c, HEAD_DIM))
    qw_b = jnp.broadcast_to(q_w[:, :, None], (h, c, HEAD_DIM))
    cd_b = jnp.broadcast_to(chunk_decay[:, None, None], (h, 8, HEAD_DIM))

    rows = c * RET_CHUNKS_PER_STEP if s % (c * RET_CHUNKS_PER_STEP) == 0 else c

    def col(off):
        return pl.BlockSpec((rows, HEAD_DIM), lambda hh, n: (n, off + hh))

    def per_head(shape):
        return pl.BlockSpec((1,) + shape, lambda hh, n: (hh, 0, 0))

    return pl.pallas_call(
        _retention_kernel,
        grid=(h, s // rows),
        in_specs=[col(0), col(h), col(2 * h), col(3 * h),
                  per_head((c, c)), per_head((c, HEAD_DIM)), per_head((c, HEAD_DIM)),
                  per_head((8, HEAD_DIM)),
                  pl.BlockSpec((1, HEAD_DIM), lambda hh, n: (0, hh))],
        out_specs=pl.BlockSpec((rows, HEAD_DIM), lambda hh, n: (n, hh)),
        out_shape=jax.ShapeDtypeStruct((s, h * HEAD_DIM), BF16),
        scratch_shapes=[pltpu.VMEM((HEAD_DIM, HEAD_DIM), F32)],
        compiler_params=_params(2),
        name="retention",
    )(zb, zb, zb, zb, intra, kw_b, qw_b, cd_b, ret_gn.reshape(1, -1).astype(F32))


def _forget_cumsum_kernel(fb_ref, fl_ref, o_ref):
    x = fl_ref[0] + fb_ref[pl.program_id(0)]
    logf = jnp.minimum(x, 0.0) - jnp.log1p(jnp.exp(-jnp.abs(x)))
    rows = x.shape[0]
    upper = (lax.broadcasted_iota(jnp.int32, (128, 128), 0)
             <= lax.broadcasted_iota(jnp.int32, (128, 128), 1)).astype(F32)
    within = _dot_exact(logf, upper)
    totals = jnp.broadcast_to(within[:, 127:128], (rows, 128))
    strict_lower = (lax.broadcasted_iota(jnp.int32, (rows, rows), 1)
                    < lax.broadcasted_iota(jnp.int32, (rows, rows), 0)).astype(F32)
    o_ref[0] = (within + _dot_exact(strict_lower, totals)) * LOG2E


def forget_cumsum_call(fl_t, fox_fb):
    h, s = fl_t.shape
    rows = s // 128
    out = pl.pallas_call(
        _forget_cumsum_kernel,
        grid=(h,),
        in_specs=[pl.BlockSpec(memory_space=pltpu.SMEM),
                  pl.BlockSpec((1, rows, 128), lambda i: (i, 0, 0))],
        out_specs=pl.BlockSpec((1, rows, 128), lambda i: (i, 0, 0)),
        out_shape=jax.ShapeDtypeStruct((h, rows, 128), F32),
        compiler_params=_params(1),
        name="forget_cumsum",
    )(fox_fb.astype(F32), fl_t.reshape(h, rows, 128))
    return out.reshape(h, s)


def _online_softmax_step(s, offset, v_ones, m_ref, acc_ref, idx, guard_empty_rows=False):
    logits = s if callable(s) else (lambda which: s)
    m_old = m_ref[idx]
    m_new = jnp.maximum(m_old, jnp.max(logits(0), -1, keepdims=True) + offset)
    m_fin = jnp.where(m_new == NEG_INF, 0.0, m_new) if guard_empty_rows else m_new
    alpha = jnp.exp2(m_old - m_fin)
    s2 = logits(1)
    p = jnp.exp2(s2 - jnp.tile(m_fin - offset, (1, s2.shape[1] // LANES)))
    acc_ref[idx] = jnp.tile(alpha, (1, 2)) * acc_ref[idx] + _dot(p.astype(BF16), v_ones)
    m_ref[idx] = m_new


def _reread(buf, slot, which, zero_ref):
    if which == 0:
        return buf[slot]
    rows = buf.shape[1]
    return buf[slot, pl.ds(pl.multiple_of(zero_ref[0], rows), rows), :]


def _init_softmax_state(m_ref, acc_ref):
    m_ref[...] = jnp.full(m_ref.shape, NEG_INF, F32)
    acc_ref[...] = jnp.zeros(acc_ref.shape, F32)


def _with_ones(v):
    return jnp.concatenate([v, jnp.ones(v.shape, v.dtype)], axis=1)


def _normalised(acc):
    return acc[:, :HEAD_DIM] / acc[:, HEAD_DIM:]


def _tile_rows_cols(t):
    return (lax.broadcasted_iota(jnp.int32, (t, t), 0),
            lax.broadcasted_iota(jnp.int32, (t, t), 1))


def _sweep_tiles_pipelined(qi, has_near, scores_into, consume, buf0, buf1, unroll):
    bufs = (buf0, buf1)
    n_far = jnp.maximum(qi - 1, 0) if has_near else qi
    scores_into(0, buf0)

    def run(first, count):
        for u in range(count):
            scores_into(first + u + 1, bufs[(u + 1) % 2])
            consume(first + u, bufs[u % 2], "far")

    def group(j, carry):
        run(unroll * j, unroll)
        return carry

    lax.fori_loop(0, n_far // unroll, group, 0)
    rest = n_far % unroll
    step = unroll // 2
    while step >= 1:
        taken = rest - rest % (2 * step)

        @pl.when(rest % (2 * step) >= step)
        def _(first=n_far - rest + taken, count=step):
            run(first, count)

        step //= 2

    for parity in range(2):
        cur, nxt = bufs[parity], bufs[1 - parity]
        if has_near:
            @pl.when((qi >= 1) & (n_far % 2 == parity))
            def _(cur=cur, nxt=nxt):
                scores_into(qi, nxt)
                consume(qi - 1, cur, "near")
                consume(qi, nxt, "diag")
        else:
            @pl.when(n_far % 2 == parity)
            def _(cur=cur):
                consume(qi, cur, "diag")
    if has_near:
        @pl.when(qi == 0)
        def _():
            consume(qi, buf0, "diag")


def _causal_keep(t):
    return lax.broadcasted_iota(jnp.int32, (t, t), 1) <= lax.broadcasted_iota(jnp.int32, (t, t), 0)


def _key_tile(ref, ki, t):
    return ref[pl.ds(pl.multiple_of(ki * t, t), t), :]


def _fox_kernel(q_ref, k_ref, v_ref, ccol_ref, crow_ref, o_ref, m_ref, acc_ref, buf0, buf1, *, t):
    qi = pl.program_id(1)
    q = q_ref[...]
    cq = jnp.broadcast_to(ccol_ref[0], (t, LANES))
    _init_softmax_state(m_ref, acc_ref)

    def scores_into(ki, buf):
        buf[0] = _dot_nt(q, _key_tile(k_ref, ki, t))

    def consume(ki, buf, kind):
        c_keys = crow_ref[0, ki]
        s = buf[0] - c_keys
        if kind == "diag":
            s = jnp.where(_causal_keep(t), s, NEG_INF)
        _online_softmax_step(s, cq, _with_ones(_key_tile(v_ref, ki, t)), m_ref, acc_ref, 0)

    _sweep_tiles_pipelined(qi, False, scores_into, consume, buf0, buf1, SWEEP_UNROLL)
    o_ref[...] = _normalised(acc_ref[0]).astype(o_ref.dtype)


def fox_attention_call(zb, c, q_off, k_off, v_off, t):
    s = zb.shape[0]
    h = FOX_HEADS
    nt = s // t
    c_col = c.reshape(h, s, 1)
    c_row = c.reshape(h, nt, 1, t)
    return pl.pallas_call(
        functools.partial(_fox_kernel, t=t),
        grid=(h, nt),
        in_specs=[pl.BlockSpec((t, HEAD_DIM), lambda hh, i: (i, q_off + hh)),
                  pl.BlockSpec((s, HEAD_DIM), lambda hh, i: (0, k_off + hh)),
                  pl.BlockSpec((s, HEAD_DIM), lambda hh, i: (0, v_off + hh)),
                  pl.BlockSpec((1, t, 1), lambda hh, i: (hh, i, 0)),
                  pl.BlockSpec((1, nt, 1, t), lambda hh, i: (hh, 0, 0, 0))],
        out_specs=pl.BlockSpec((t, HEAD_DIM), lambda hh, i: (i, hh)),
        out_shape=jax.ShapeDtypeStruct((s, h * HEAD_DIM), BF16),
        scratch_shapes=[pltpu.VMEM((1, t, LANES), F32), pltpu.VMEM((1, t, 2 * HEAD_DIM), F32),
                        pltpu.VMEM((1, t, t), F32), pltpu.VMEM((1, t, t), F32)],
        compiler_params=_params(2),
        name="fox_attention",
    )(zb, zb, zb, c_col, c_row)


def _t5_bias(tbl_ref, head, dist):
    bias = jnp.full(dist.shape, tbl_ref[0, head], F32)
    for b in range(1, T5_BUCKETS):
        bias = jnp.where(dist >= T5_THRESH[b], tbl_ref[b, head], bias)
    return bias


def _fill_t5_tiles(tbl_ref, head, bias_ref, slot, t):
    rows, cols = _tile_rows_cols(t)
    dist = rows - cols
    bias_ref[slot, 0] = _t5_bias(tbl_ref, head, dist) * LOG2E
    bias_ref[slot, 1] = _t5_bias(tbl_ref, head, dist + t) * LOG2E


def _diff_kernel(tbl_ref, zero_ref, q_ref, k_ref, v_ref, lam_ref, g_ref, o_ref,
                 m_ref, acc_ref, bias_ref, buf0, buf1, *, t, lambda_init):
    head = pl.program_id(0)
    qi = pl.program_id(1)

    @pl.when(qi == 0)
    def _():
        _fill_t5_tiles(tbl_ref, head, bias_ref, 0, t)

    q = q_ref[...]
    lane = lax.broadcasted_iota(jnp.int32, q.shape, 1)
    zero = jnp.zeros_like(q)
    q_maps = (jnp.where(lane < HEAD_DIM // 2, q, zero), jnp.where(lane >= HEAD_DIM // 2, q, zero))
    _init_softmax_state(m_ref, acc_ref)
    far_bias = tbl_ref[T5_BUCKETS - 1, head] * LOG2E

    def scores_into(ki, buf):
        k = _key_tile(k_ref, ki, t)
        for mi in range(2):
            buf[mi] = _dot_nt(q_maps[mi], k)

    def consume(ki, buf, kind):
        v_ones = _with_ones(_key_tile(v_ref, ki, t))
        for mi in range(2):
            if kind == "far":
                _online_softmax_step(lambda which, mi=mi: _reread(buf, mi, which, zero_ref),
                                     far_bias, v_ones, m_ref, acc_ref, mi)
                continue
            s = buf[mi] + bias_ref[0, 1 if kind == "near" else 0]
            if kind == "diag":
                s = jnp.where(_causal_keep(t), s, NEG_INF)
            _online_softmax_step(s, 0.0, v_ones, m_ref, acc_ref, mi)

    _sweep_tiles_pipelined(qi, True, scores_into, consume, buf0, buf1, SWEEP_UNROLL)

    lam = lam_ref[...]
    lmbda = (jnp.exp(jnp.sum(lam[0:1] * lam[1:2], keepdims=True))
             - jnp.exp(jnp.sum(lam[2:3] * lam[3:4], keepdims=True)) + lambda_init)
    o = _normalised(acc_ref[0]) - lmbda * _normalised(acc_ref[1])
    y = o * lax.rsqrt(jnp.mean(o * o, -1, keepdims=True) + NORM_EPS) * g_ref[...]
    o_ref[...] = (y * (1.0 - lambda_init)).astype(o_ref.dtype)


def diff_attention_call(zb, t5_table, diff_lambda, subln, lambda_init, q_off, k_off, v_off, t):
    s = zb.shape[0]
    h = DIFF_HEADS
    nt = s // t
    return pl.pallas_call(
        functools.partial(_diff_kernel, t=t, lambda_init=lambda_init),
        grid=(h, nt),
        in_specs=[pl.BlockSpec(memory_space=pltpu.SMEM),
                  pl.BlockSpec(memory_space=pltpu.SMEM),
                  pl.BlockSpec((t, HEAD_DIM), lambda hh, i: (i, q_off + hh)),
                  pl.BlockSpec((s, HEAD_DIM), lambda hh, i: (0, k_off + hh)),
                  pl.BlockSpec((s, HEAD_DIM), lambda hh, i: (0, v_off + hh)),
                  pl.BlockSpec(diff_lambda.shape, lambda hh, i: (0, 0)),
                  pl.BlockSpec((1, HEAD_DIM), lambda hh, i: (0, 0))],
        out_specs=pl.BlockSpec((t, HEAD_DIM), lambda hh, i: (i, hh)),
        out_shape=jax.ShapeDtypeStruct((s, h * HEAD_DIM), BF16),
        scratch_shapes=[pltpu.VMEM((2, t, LANES), F32), pltpu.VMEM((2, t, 2 * HEAD_DIM), F32),
                        pltpu.VMEM((1, 2, t, t), F32),
                        pltpu.VMEM((2, t, t), F32), pltpu.VMEM((2, t, t), F32)],
        compiler_params=_params(2),
        name="diff_attention",
    )(t5_table.astype(F32), jnp.zeros((1,), jnp.int32), zb, zb, zb, diff_lambda.astype(F32),
      subln.reshape(1, -1).astype(F32))


def _compress_kernel(kr_ref, pos_ref, w1_ref, w2_ref, o_ref):
    half = w1_ref.shape[1] // 2
    kr = kr_ref[0, 0]
    w_top = w1_ref[0, :half, :]
    w_bot = w1_ref[0, half:, :]
    pos = pos_ref[0].astype(BF16)
    first = _dot(kr, w_top)
    second = _dot(kr, w_bot)
    pos_part = _dot(pos[:, :half], w_top) + _dot(pos[:, half:], w_bot)
    rows = kr.shape[0]
    second_next = pltpu.roll(second, rows - 1, 0)
    hid = first + second_next + pos_part[0:1, :]
    hid = hid * jax.nn.sigmoid(hid)
    o_ref[0, 0] = _dot(hid.astype(BF16), w2_ref[0]).astype(o_ref.dtype)


def compress_call(kv_rows, pos, w1, w2):
    two, g, nc, width = kv_rows.shape
    return pl.pallas_call(
        _compress_kernel,
        grid=(two, g),
        in_specs=[pl.BlockSpec((1, 1, nc, width), lambda a, b: (a, b, 0, 0)),
                  pl.BlockSpec((1, 16, 2 * width), lambda a, b: (a, 0, 0)),
                  pl.BlockSpec((1, 2 * width, HEAD_DIM), lambda a, b: (a, 0, 0)),
                  pl.BlockSpec((1, HEAD_DIM, HEAD_DIM), lambda a, b: (a, 0, 0))],
        out_specs=pl.BlockSpec((1, 1, nc, HEAD_DIM), lambda a, b: (a, b, 0, 0)),
        out_shape=jax.ShapeDtypeStruct((two, g, nc, HEAD_DIM), BF16),
        compiler_params=_params(2),
        name="nsa_compress",
    )(kv_rows, pos, w1, w2)


def _nsa_cmp_kernel(tbl_ref, q_ref, kc_ref, vc_ref, gate_ref, o_ref, sel_ref, qext_ref, *,
                    head0, qb0):
    step = pl.program_id(0)
    qb = qb0 + step
    tq = q_ref.shape[0]
    ncp = kc_ref.shape[2]
    ns = sel_ref.shape[2]
    n_heads = NSA_KV_HEADS * NSA_GROUP

    @pl.when(step == 0)
    def _():
        rows = lax.broadcasted_iota(jnp.int32, (tq, LANES), 0)
        lane = lax.broadcasted_iota(jnp.int32, (tq, LANES), 1)
        slot = lane % CMP_BAND
        piece = lane // CMP_BAND
        dist = rows - (slot - CMP_BAND_BACK) * CMP_STRIDE - (CMP_LEN - 1)
        for hd in range(n_heads):
            far = tbl_ref[T5_BUCKETS - 1, head0 + hd]
            delta = (_t5_bias(tbl_ref, head0 + hd, dist) - far) * LOG2E
            hi, mid, lo = (x.astype(F32) for x in _split_bf16(delta))
            packed = jnp.where(piece == 0, hi, jnp.where(piece == 1, mid, jnp.where(piece == 2, lo, 0.0)))
            qext_ref[hd] = packed.astype(BF16)

    t_pos = qb * tq + lax.broadcasted_iota(jnp.int32, (tq, ncp), 0)
    cmp_end = lax.broadcasted_iota(jnp.int32, (tq, ncp), 1) * CMP_STRIDE + (CMP_LEN - 1)
    visible = cmp_end <= t_pos

    first_tok = qb * (tq // CMP_STRIDE) - CMP_BAND_BACK
    tok = lax.broadcasted_iota(jnp.int32, (ncp, LANES), 0)
    klane = lax.broadcasted_iota(jnp.int32, (ncp, LANES), 1)
    in_slot = (klane < 3 * CMP_BAND) & (tok == first_tok + klane % CMP_BAND)
    kext = jnp.where(in_slot, 1.0, 0.0).astype(BF16)

    ci = lax.broadcasted_iota(jnp.int32, (ncp, ns), 0) * CMP_STRIDE
    bj = lax.broadcasted_iota(jnp.int32, (ncp, ns), 1) * SLC_LEN
    overlap = jnp.where((ci <= bj + SLC_LEN - 1) & (ci + CMP_LEN - 1 >= bj), 1.0, 0.0).astype(BF16)
    t_sel = qb * tq + lax.broadcasted_iota(jnp.int32, (tq, ns), 0)
    blk = lax.broadcasted_iota(jnp.int32, (tq, ns), 1)
    cur = t_sel // SLC_LEN
    forced = (blk == 0) | (blk == cur) | (blk == cur - 1)
    valid = blk * SLC_LEN <= t_sel

    scores = []
    for grp in range(NSA_KV_HEADS):
        k_aug = jnp.concatenate([kc_ref[0, grp], kext], axis=1)
        vc = vc_ref[0, grp]
        gates = jax.nn.sigmoid(gate_ref[0, grp])
        p_sum = jnp.zeros((tq, ncp), F32)
        for r in range(NSA_GROUP):
            hd = grp * NSA_GROUP + r
            cols = slice(hd * HEAD_DIM, (hd + 1) * HEAD_DIM)
            q_aug = jnp.concatenate([q_ref[:, cols], qext_ref[hd]], axis=1)
            far2 = tbl_ref[T5_BUCKETS - 1, head0 + hd] * LOG2E
            s = jnp.where(visible, _dot_nt(q_aug, k_aug), NEG_INF)
            m = jnp.max(s, -1, keepdims=True) + far2
            m = jnp.where(m == NEG_INF, 0.0, m)
            e = jnp.exp2(s - (m - far2))
            p = e * (1.0 / jnp.maximum(jnp.sum(e, -1, keepdims=True), 1e-30))
            p_sum = p_sum + p
            o = _dot(p.astype(BF16), vc) * gates[:, r:r + 1]
            o_ref[:, cols] = o.astype(o_ref.dtype)
        p_hi, p_mid, p_lo = _split_bf16(p_sum)
        imp = _dot(p_hi, overlap) + _dot(p_mid, overlap) + _dot(p_lo, overlap)
        scores.append(jnp.where(forced, SEL_BIG, jnp.where(valid, imp, -SEL_BIG)))

    score = jnp.concatenate(scores, axis=0)
    blk_f = jnp.concatenate([blk] * NSA_KV_HEADS, axis=0).astype(F32)

    def pick(_, carry):
        sc, chosen = carry
        best = jnp.max(sc, -1, keepdims=True)
        first = jnp.min(jnp.where(sc == best, blk_f, float(ns)), -1, keepdims=True)
        hit = blk_f == first
        return jnp.where(hit, NEG_INF, sc), jnp.where(hit, 1.0, chosen)

    _, chosen = lax.fori_loop(0, min(SLC_TOPN, ns), pick, (score, jnp.zeros(score.shape, F32)))
    for grp in range(NSA_KV_HEADS):
        sel_ref[grp] = chosen[grp * tq:(grp + 1) * tq].astype(sel_ref.dtype)


def nsa_cmp_call(zb, kvc, gates, t5_table, q_off, ns):
    s = zb.shape[0]
    g = NSA_KV_HEADS
    tq = CMP_QBLOCK
    ncp = kvc.shape[2]
    width = NSA_HEADS * HEAD_DIM
    n_ranges = next(r for r in (4, 2, 1) if ncp % (r * LANES) == 0)
    steps = s // tq // n_ranges
    outs = []
    for c in range(n_ranges):
        ncp_c = ncp * (c + 1) // n_ranges
        first = c * steps
        outs.append(pl.pallas_call(
            functools.partial(_nsa_cmp_kernel, head0=DIFF_HEADS, qb0=first),
            grid=(steps,),
            in_specs=[pl.BlockSpec(memory_space=pltpu.SMEM),
                      pl.BlockSpec((tq, width), lambda i, first=first: (first + i, q_off)),
                      pl.BlockSpec((1, g, ncp_c, HEAD_DIM), lambda i: (0, 0, 0, 0)),
                      pl.BlockSpec((1, g, ncp_c, HEAD_DIM), lambda i: (1, 0, 0, 0)),
                      pl.BlockSpec((1, g, tq, NSA_GROUP), lambda i, first=first: (0, 0, first + i, 0))],
            out_specs=[pl.BlockSpec((tq, width), lambda i: (i, 0)),
                       pl.BlockSpec((g, tq, ns), lambda i: (0, i, 0))],
            out_shape=[jax.ShapeDtypeStruct((steps * tq, width), BF16),
                       jax.ShapeDtypeStruct((g, steps * tq, ns), BF16)],
            scratch_shapes=[pltpu.VMEM((NSA_HEADS, tq, LANES), BF16)],
            compiler_params=_params(1),
            name="nsa_compressed",
        )(t5_table.astype(F32), zb, kvc, kvc, gates))
    o_cmp = jnp.concatenate([o for o, _ in outs], axis=0)
    sel = jnp.concatenate([m for _, m in outs], axis=1)
    return o_cmp, sel


def _nsa_sweep_kernel(tbl_ref, q_ref, k_ref, v_ref, gate_ref, *rest, t, mode, head0, branch):
    if mode == "selected":
        sel_ref, o_ref, m_ref, acc_ref, bias_ref, buf0, buf1 = rest
    else:
        o_ref, m_ref, acc_ref, bias_ref, buf0, buf1 = rest
    grp = pl.program_id(0)
    qi = pl.program_id(1)

    @pl.when(qi == 0)
    def _():
        for r in range(NSA_GROUP):
            _fill_t5_tiles(tbl_ref, head0 + grp * NSA_GROUP + r, bias_ref, r, t)

    _init_softmax_state(m_ref, acc_ref)

    def scores_into(ki, buf):
        k = _key_tile(k_ref, ki, t)
        for r in range(NSA_GROUP):
            buf[r] = _dot_nt(q_ref[:, r * HEAD_DIM:(r + 1) * HEAD_DIM], k)
        if mode == "selected":
            ns = sel_ref.shape[2]
            blk_of_key = (ki * t + lax.broadcasted_iota(jnp.int32, (ns, t), 1)) // SLC_LEN
            expand = jnp.where(lax.broadcasted_iota(jnp.int32, (ns, t), 0) == blk_of_key, 1.0, 0.0)
            buf[NSA_GROUP] = _dot(sel_ref[0], expand.astype(BF16))

    def consume(ki, buf, kind):
        v_ones = _with_ones(_key_tile(v_ref, ki, t))
        if mode == "selected":
            keep = buf[NSA_GROUP] > 0.5
            if kind == "diag":
                keep = keep & _causal_keep(t)
        elif kind == "diag":
            keep = _causal_keep(t)
        else:
            keep = jnp.logical_not(_causal_keep(t))
        for r in range(NSA_GROUP):
            s = buf[r]
            if kind != "far":
                s = s + bias_ref[r, 1 if kind == "near" else 0]
            far_bias = tbl_ref[T5_BUCKETS - 1, head0 + grp * NSA_GROUP + r] * LOG2E
            _online_softmax_step(jnp.where(keep, s, NEG_INF), far_bias if kind == "far" else 0.0,
                                 v_ones, m_ref, acc_ref, r, guard_empty_rows=(mode == "window"))

    if mode == "selected":
        _sweep_tiles_pipelined(qi, True, scores_into, consume, buf0, buf1, SWEEP_UNROLL_4HEADS)
    else:
        @pl.when(qi >= 1)
        def _():
            scores_into(qi - 1, buf0)
            consume(qi - 1, buf0, "near")

        scores_into(qi, buf1)
        consume(qi, buf1, "diag")

    gates = jax.nn.sigmoid(gate_ref[0, 0])
    for r in range(NSA_GROUP):
        o = _normalised(acc_ref[r]) * gates[:, r:r + 1]
        o_ref[:, r * HEAD_DIM:(r + 1) * HEAD_DIM] = o.astype(o_ref.dtype)


def nsa_sweep_call(zb, gates, t5_table, sel, q_off, k_off, v_off, t, mode):
    s = zb.shape[0]
    g = NSA_KV_HEADS
    nt = s // t
    width = NSA_GROUP * HEAD_DIM
    branch = 1 if mode == "selected" else 2
    n_buf = NSA_GROUP + 1 if mode == "selected" else NSA_GROUP
    if mode == "window":
        assert t == WINDOW
    in_specs = [pl.BlockSpec(memory_space=pltpu.SMEM),
                pl.BlockSpec((t, width), lambda gg, i: (i, q_off + gg)),
                pl.BlockSpec((s, HEAD_DIM), lambda gg, i: (0, k_off + gg)),
                pl.BlockSpec((s, HEAD_DIM), lambda gg, i: (0, v_off + gg)),
                pl.BlockSpec((1, 1, t, NSA_GROUP), lambda gg, i: (branch, gg, i, 0))]
    args = [t5_table.astype(F32), zb, zb, zb, gates]
    if mode == "selected":
        ns = sel.shape[2]
        in_specs.append(pl.BlockSpec((1, t, ns), lambda gg, i: (gg, i, 0)))
        args.append(sel)
    return pl.pallas_call(
        functools.partial(_nsa_sweep_kernel, t=t, mode=mode, head0=DIFF_HEADS, branch=branch),
        grid=(g, nt),
        in_specs=in_specs,
        out_specs=pl.BlockSpec((t, width), lambda gg, i: (i, gg)),
        out_shape=jax.ShapeDtypeStruct((s, g * width), BF16),
        scratch_shapes=[pltpu.VMEM((NSA_GROUP, t, LANES), F32),
                        pltpu.VMEM((NSA_GROUP, t, 2 * HEAD_DIM), F32),
                        pltpu.VMEM((NSA_GROUP, 2, t, t), F32),
                        pltpu.VMEM((n_buf, t, t), F32), pltpu.VMEM((n_buf, t, t), F32)],
        compiler_params=_params(2),
        name="nsa_" + mode,
    )(*args)


def _gate_columns(w_in, layer, start):
    w = w_in[layer, :, start:]
    return jnp.pad(w, ((0, 0), (0, LANES - w.shape[1])))


def _column_tile(n):
    return next(tn for tn in (1024, 1408, 512, 256, 128) if n % tn == 0)


def _query_column_scale(n, spans):
    cs = np.ones((1, n), np.float32)
    for start, stop, width in spans:
        cs[0, start:stop] = width ** -0.5 * LOG2E
    return jnp.asarray(cs)


def even_mixer(x, norm_g, w_in, layer, ret_gn, fox_fb, w_out, t):
    main = 7 * RET_HEADS * HEAD_DIM
    fox_q = 4 * RET_HEADS * HEAD_DIM
    col_scale = _query_column_scale(main, [(fox_q, fox_q + FOX_HEADS * HEAD_DIM, HEAD_DIM)])
    zb, zg = norm_proj_call(x, norm_g, w_in, layer, col_scale, _gate_columns(w_in, layer, main),
                            main, _column_tile(main))
    ret_out = retention_call(zb, ret_gn)
    c = forget_cumsum_call(zg[:, :FOX_HEADS].T, fox_fb)
    fox_out = fox_attention_call(zb, c, 4 * RET_HEADS, 5 * RET_HEADS, 6 * RET_HEADS, t)
    return residual_matmul_call(x, [ret_out], w_out, layer, [fox_out], tn=1024)


def odd_mixer(x, norm_g, w_in, layer, diff_lambda, diff_subln, cmp_pos, cmp_w1, cmp_w2, w_out,
              t5_table, lambda_init, t):
    s = x.shape[0]
    main = 5632
    hd = HEAD_DIM
    nsa_q = 3 * DIFF_HEADS * hd
    col_scale = _query_column_scale(main, [(0, DIFF_HEADS * hd, hd // 2),
                                           (nsa_q, nsa_q + NSA_HEADS * hd, hd)])
    zb, zg = norm_proj_call(x, norm_g, w_in, layer, col_scale, _gate_columns(w_in, layer, main),
                            main, _column_tile(main))
    diff_out = diff_attention_call(zb, t5_table, diff_lambda, diff_subln, lambda_init, 0, 8, 16, t)
    g = NSA_KV_HEADS
    ckv = zb[:, 32 * hd:36 * hd].reshape(s // CMP_STRIDE, CMP_STRIDE, 2, g, hd)
    kv_rows = ckv.transpose(2, 3, 0, 1, 4).reshape(2, g, s // CMP_STRIDE, CMP_STRIDE * hd)
    pos = jnp.broadcast_to(cmp_pos.reshape(2, 1, CMP_LEN * hd), (2, 16, CMP_LEN * hd)).astype(F32)
    kvc = compress_call(kv_rows, pos, cmp_w1.astype(BF16), cmp_w2.astype(BF16))
    gates = zg[:, :3 * NSA_HEADS].reshape(s, 3, g, NSA_GROUP).transpose(1, 2, 0, 3)
    ns = s // SLC_LEN
    o_cmp, sel = nsa_cmp_call(zb, kvc, gates, t5_table, 3, ns)
    o_slc = nsa_sweep_call(zb, gates, t5_table, sel, 6, 36, 38, t, "selected")
    o_win = nsa_sweep_call(zb, gates, t5_table, None, 6, 40, 42, WINDOW, "window")
    return residual_matmul_call(x, [diff_out], w_out, layer, [o_cmp, o_slc, o_win], tn=1024)


def ffn_and_ple(x, norm_g, w_gate, w_up, w_down, p, ple_gate, ple_proj, layer):
    u = norm_swiglu_call(x, norm_g, w_gate, w_up, layer)
    x2, x2b = residual_matmul_call(x, [u], w_down, layer, with_bf16=True)
    return ple_call(x2, x2b, ple_gate, p, ple_proj, layer, tn=1024)


def _attention_tile(s):
    return min(512, s)


def kernel(x, p, norm_mix, norm_ffn, w_in_even, ret_gn, fox_fb, w_out_even, w_in_odd, diff_lambda, diff_subln, cmp_pos, cmp_w1, cmp_w2, w_out_odd, t5_table, ffn_gate, ffn_up, ffn_down, ple_gate, ple_proj, final_norm):
    batch, s, d = x.shape
    assert batch == 1
    depth = p.shape[0]
    t = _attention_tile(s)
    xs = x[0]
    (w_in_even, w_out_even, w_in_odd, w_out_odd, ffn_gate, ffn_up, ffn_down, ple_gate, ple_proj) = (
        w.astype(BF16) for w in (w_in_even, w_out_even, w_in_odd, w_out_odd, ffn_gate, ffn_up,
                                 ffn_down, ple_gate, ple_proj))
    for i in range(depth):
        j = i // 2
        if i % 2 == 0:
            xs = even_mixer(xs, norm_mix[i], w_in_even, j, ret_gn[j], fox_fb[j], w_out_even, t)
        else:
            lambda_init = 0.8 - 0.6 * math.exp(-0.3 * i)
            xs = odd_mixer(xs, norm_mix[i], w_in_odd, j, diff_lambda[j], diff_subln[j], cmp_pos[j],
                           cmp_w1[j], cmp_w2[j], w_out_odd, t5_table, lambda_init, t)
        xs = ffn_and_ple(xs, norm_ffn[i], ffn_gate, ffn_up, ffn_down, p, ple_gate, ple_proj, i)
    out = rmsnorm_call(xs, final_norm, F32)
    return out[None]
```

```python
import functools
import math

import numpy as np
import jax
import jax.numpy as jnp
from jax import lax
from jax.experimental import pallas as pl
from jax.experimental.pallas import tpu as pltpu

F32 = jnp.float32
BF16 = jnp.bfloat16

HEAD_DIM = 128
NORM_EPS = 1e-6
RET_HEADS = 8
FOX_HEADS = 8
DIFF_HEADS = 8
NSA_HEADS = 8
NSA_KV_HEADS = 2
NSA_GROUP = NSA_HEADS // NSA_KV_HEADS
RET_CHUNK = 128
CMP_LEN = 32
CMP_STRIDE = 16
SLC_LEN = 64
SLC_TOPN = 16
WINDOW = 512
SEL_BIG = 1e9
T5_BUCKETS = 32
T5_MAX_DIST = 128
PLE_DIM = 256
CMP_QBLOCK = 128
CMP_BAND = 16
CMP_BAND_BACK = 9

_T5_EXACT = T5_BUCKETS // 2
T5_THRESH = tuple(
    b if b <= _T5_EXACT else int(math.ceil(
        _T5_EXACT * (T5_MAX_DIST / _T5_EXACT) ** ((b - _T5_EXACT) / (T5_BUCKETS - _T5_EXACT))))
    for b in range(T5_BUCKETS))

VMEM_LIMIT = 56 * 1024 * 1024
NEG_INF = float("-inf")
LANES = 128
SWEEP_UNROLL = 8
SWEEP_UNROLL_4HEADS = 4
RET_CHUNKS_PER_STEP = 4
LOG2E = math.log2(math.e)


def _params(n_axes):
    return pltpu.CompilerParams(dimension_semantics=("arbitrary",) * n_axes,
                                vmem_limit_bytes=VMEM_LIMIT)


def _dot(a, b):
    return jnp.dot(a, b, preferred_element_type=F32)


def _dot_nt(a, b):
    return lax.dot_general(a, b, (((1,), (1,)), ((), ())), preferred_element_type=F32)


def _dot_exact(a, b):
    return jnp.dot(a, b, preferred_element_type=F32, precision=lax.Precision.HIGHEST)


def _split_bf16(x):
    hi = x.astype(BF16)
    r1 = x - hi.astype(F32)
    mid = r1.astype(BF16)
    lo = (r1 - mid.astype(F32)).astype(BF16)
    return hi, mid, lo


def _rmsnorm_kernel(x_ref, g_ref, o_ref):
    x = x_ref[...]
    y = x * lax.rsqrt(jnp.mean(x * x, -1, keepdims=True) + NORM_EPS) * g_ref[...]
    o_ref[...] = y.astype(o_ref.dtype)


def rmsnorm_call(x, g, out_dtype, tm=512):
    m, d = x.shape
    tm = min(tm, m)
    return pl.pallas_call(
        _rmsnorm_kernel,
        grid=(m // tm,),
        in_specs=[pl.BlockSpec((tm, d), lambda i: (i, 0)),
                  pl.BlockSpec((1, d), lambda i: (0, 0))],
        out_specs=pl.BlockSpec((tm, d), lambda i: (i, 0)),
        out_shape=jax.ShapeDtypeStruct((m, d), out_dtype),
        compiler_params=_params(1),
        name="rmsnorm",
    )(x, g.reshape(1, d).astype(F32))


def _weight_spec(k, tn, layer, row_block=0):
    return pl.BlockSpec((None, k, tn), lambda i, j: (layer, row_block, j))


def _normed_rows_into(x_ref, g_ref, a_ref):
    x = x_ref[...]
    y = x * lax.rsqrt(jnp.mean(x * x, -1, keepdims=True) + NORM_EPS) * g_ref[...]
    a_ref[...] = y.astype(a_ref.dtype)


def _norm_proj_kernel(x_ref, g_ref, w_ref, cs_ref, wgate_ref, o_ref, gate_ref, a_ref):
    @pl.when(pl.program_id(1) == 0)
    def _():
        _normed_rows_into(x_ref, g_ref, a_ref)
        gate_ref[...] = _dot(a_ref[...], wgate_ref[...])

    o_ref[...] = (_dot(a_ref[...], w_ref[...]) * cs_ref[...]).astype(o_ref.dtype)


def norm_proj_call(x, g, w, layer, col_scale, w_gate, n, tn, tm=1024):
    m, k = x.shape
    tm = min(tm, m)
    assert n % tn == 0
    return pl.pallas_call(
        _norm_proj_kernel,
        grid=(m // tm, n // tn),
        in_specs=[pl.BlockSpec((tm, k), lambda i, j: (i, 0)),
                  pl.BlockSpec((1, k), lambda i, j: (0, 0)),
                  _weight_spec(k, tn, layer),
                  pl.BlockSpec((1, tn), lambda i, j: (0, j)),
                  pl.BlockSpec((k, LANES), lambda i, j: (0, 0))],
        out_specs=[pl.BlockSpec((tm, tn), lambda i, j: (i, j)),
                   pl.BlockSpec((tm, LANES), lambda i, j: (i, 0))],
        out_shape=[jax.ShapeDtypeStruct((m, n), BF16), jax.ShapeDtypeStruct((m, LANES), F32)],
        scratch_shapes=[pltpu.VMEM((tm, k), BF16)],
        compiler_params=_params(2),
        name="norm_proj",
    )(x, g.reshape(1, k).astype(F32), w, col_scale, w_gate)


def _norm_swiglu_kernel(x_ref, g_ref, wg_ref, wu_ref, o_ref, a_ref):
    @pl.when(pl.program_id(1) == 0)
    def _():
        _normed_rows_into(x_ref, g_ref, a_ref)

    a = a_ref[...]
    gate = _dot(a, wg_ref[...])
    up = _dot(a, wu_ref[...])
    o_ref[...] = (gate * jax.nn.sigmoid(gate) * up).astype(o_ref.dtype)


def norm_swiglu_call(x, g, wg, wu, layer, tm=1024, tn=512):
    m, k = x.shape
    n = wg.shape[2]
    tm, tn = min(tm, m), min(tn, n)
    return pl.pallas_call(
        _norm_swiglu_kernel,
        grid=(m // tm, n // tn),
        in_specs=[pl.BlockSpec((tm, k), lambda i, j: (i, 0)),
                  pl.BlockSpec((1, k), lambda i, j: (0, 0)),
                  _weight_spec(k, tn, layer),
                  _weight_spec(k, tn, layer)],
        out_specs=pl.BlockSpec((tm, tn), lambda i, j: (i, j)),
        out_shape=jax.ShapeDtypeStruct((m, n), BF16),
        scratch_shapes=[pltpu.VMEM((tm, k), BF16)],
        compiler_params=_params(2),
        name="norm_swiglu",
    )(x, g.reshape(1, k).astype(F32), wg, wu)


def _mm_residual_kernel(*refs, n_a, n_b, with_bf16):
    res_ref = refs[0]
    a_refs = refs[1:1 + n_a]
    w1_ref = refs[1 + n_a]
    pos = 2 + n_a
    acc = res_ref[...]

    def summed(group):
        if len(group) == 1:
            return group[0][...]
        tot = group[0][...].astype(F32)
        for r in group[1:]:
            tot = tot + r[...].astype(F32)
        return tot.astype(BF16)

    acc = acc + _dot(summed(a_refs), w1_ref[...])
    if n_b:
        b_refs = refs[pos:pos + n_b]
        w2_ref = refs[pos + n_b]
        pos += n_b + 1
        acc = acc + _dot(summed(b_refs), w2_ref[...])
    refs[pos][...] = acc
    if with_bf16:
        refs[pos + 1][...] = acc.astype(BF16)


def residual_matmul_call(res, a_list, w, layer, b_list=(), with_bf16=False, tm=1024, tn=512):
    m, n = res.shape
    tm, tn = min(tm, m), min(tn, n)
    ka = a_list[0].shape[1]
    args = [res] + list(a_list) + [w]
    in_specs = [pl.BlockSpec((tm, tn), lambda i, j: (i, j))]
    in_specs += [pl.BlockSpec((tm, ka), lambda i, j: (i, 0)) for _ in a_list]
    in_specs += [_weight_spec(ka, tn, layer)]
    if b_list:
        kb = b_list[0].shape[1]
        assert kb == ka and w.shape[1] == ka + kb
        args += list(b_list) + [w]
        in_specs += [pl.BlockSpec((tm, kb), lambda i, j: (i, 0)) for _ in b_list]
        in_specs += [_weight_spec(kb, tn, layer, row_block=1)]
    else:
        assert w.shape[1] == ka
    out_shape = [jax.ShapeDtypeStruct((m, n), F32)]
    out_specs = [pl.BlockSpec((tm, tn), lambda i, j: (i, j))]
    if with_bf16:
        out_shape.append(jax.ShapeDtypeStruct((m, n), BF16))
        out_specs.append(pl.BlockSpec((tm, tn), lambda i, j: (i, j)))
    out = pl.pallas_call(
        functools.partial(_mm_residual_kernel, n_a=len(a_list), n_b=len(b_list), with_bf16=with_bf16),
        grid=(m // tm, n // tn),
        in_specs=in_specs,
        out_specs=out_specs,
        out_shape=out_shape,
        compiler_params=_params(2),
        name="residual_matmul",
    )(*args)
    return out if with_bf16 else out[0]


def _ple_kernel(xb_ref, xres_ref, wg_ref, p_ref, wp_ref, o_ref):
    gate = jax.nn.sigmoid(_dot(xb_ref[...], wg_ref[...]))
    emb = _dot(p_ref[...].astype(BF16), wp_ref[...])
    o_ref[...] = xres_ref[...] + gate * emb


def ple_call(x, xb, wg, p, wp, layer, tm=1024, tn=512):
    m, n = x.shape
    tm, tn = min(tm, m), min(tn, n)
    k = xb.shape[1]
    kp = p.shape[3]
    return pl.pallas_call(
        _ple_kernel,
        grid=(m // tm, n // tn),
        in_specs=[pl.BlockSpec((tm, k), lambda i, j: (i, 0)),
                  pl.BlockSpec((tm, tn), lambda i, j: (i, j)),
                  _weight_spec(k, tn, layer),
                  pl.BlockSpec((None, None, tm, kp), lambda i, j: (layer, 0, i, 0)),
                  _weight_spec(kp, tn, layer)],
        out_specs=pl.BlockSpec((tm, tn), lambda i, j: (i, j)),
        out_shape=jax.ShapeDtypeStruct((m, n), F32),
        compiler_params=_params(2),
        name="ple",
    )(xb, x, wg, p, wp)


def _retention_kernel(q_ref, k_ref, v_ref, g_ref, dec_ref, kw_ref, qw_ref, cd_ref, gn_ref,
                      o_ref, state_ref):
    @pl.when(pl.program_id(1) == 0)
    def _():
        state_ref[...] = jnp.zeros_like(state_ref)

    c = RET_CHUNK
    state = state_ref[...]
    for i in range(q_ref.shape[0] // c):
        rows = slice(i * c, (i + 1) * c)
        q = q_ref[rows, :]
        kf = k_ref[rows, :].astype(F32) * (HEAD_DIM ** -0.5)
        v = v_ref[rows, :]
        att = _dot_nt(q, kf.astype(BF16)) * dec_ref[0]
        y = _dot(att.astype(BF16), v)
        q_scaled = (q.astype(F32) * qw_ref[0]).astype(BF16)
        y = y + _dot(q_scaled, state.astype(BF16))
        k_scaled_t = (kf * kw_ref[0]).T.astype(BF16)
        state = state * cd_ref[0][0:1, :] + _dot(k_scaled_t, v)
        mu = jnp.mean(y, -1, keepdims=True)
        yc = y - mu
        var = jnp.mean(yc * yc, -1, keepdims=True)
        yn = yc * lax.rsqrt(var + NORM_EPS) * gn_ref[...]
        g = g_ref[rows, :].astype(F32)
        o_ref[rows, :] = (yn * (g * jax.nn.sigmoid(g))).astype(o_ref.dtype)
    state_ref[...] = state


def retention_call(zb, ret_gn):
    s = zb.shape[0]
    c = RET_CHUNK
    h = RET_HEADS
    log_gamma = jnp.log1p(-jnp.exp2(-5.0 - jnp.arange(h, dtype=F32)))
    pos = jnp.arange(c, dtype=F32)
    rel = pos[:, None] - pos[None, :]
    intra = jnp.where(rel >= 0, jnp.exp(log_gamma[:, None, None] * jnp.maximum(rel, 0.0)), 0.0)
    k_w = jnp.exp(log_gamma[:, None] * (c - 1 - pos))
    q_w = jnp.exp(log_gamma[:, None] * (pos + 1.0))
    chunk_decay = jnp.exp(log_gamma * c)
    kw_b = jnp.broadcast_to(k_w[:, :, None], (h, c, HEAD_DIM))
    qw_b = jnp.broadcast_to(q_w[:, :, None], (h, c, HEAD_DIM))
    cd_b = jnp.broadcast_to(chunk_decay[:, None, None], (h, 8, HEAD_DIM))

    rows = c * RET_CHUNKS_PER_STEP if s % (c * RET_CHUNKS_PER_STEP) == 0 else c

    def col(off):
        return pl.BlockSpec((rows, HEAD_DIM), lambda hh, n: (n, off + hh))

    def per_head(shape):
        return pl.BlockSpec((1,) + shape, lambda hh, n: (hh, 0, 0))

    return pl.pallas_call(
        _retention_kernel,
        grid=(h, s // rows),
        in_specs=[col(0), col(h), col(2 * h), col(3 * h),
                  per_head((c, c)), per_head((c, HEAD_DIM)), per_head((c, HEAD_DIM)),
                  per_head((8, HEAD_DIM)),
                  pl.BlockSpec((1, HEAD_DIM), lambda hh, n: (0, hh))],
        out_specs=pl.BlockSpec((rows, HEAD_DIM), lambda hh, n: (n, hh)),
        out_shape=jax.ShapeDtypeStruct((s, h * HEAD_DIM), BF16),
        scratch_shapes=[pltpu.VMEM((HEAD_DIM, HEAD_DIM), F32)],
        compiler_params=_params(2),
        name="retention",
    )(zb, zb, zb, zb, intra, kw_b, qw_b, cd_b, ret_gn.reshape(1, -1).astype(F32))


def _forget_cumsum_kernel(fb_ref, fl_ref, o_ref):
    x = fl_ref[0] + fb_ref[pl.program_id(0)]
    logf = jnp.minimum(x, 0.0) - jnp.log1p(jnp.exp(-jnp.abs(x)))
    rows = x.shape[0]
    upper = (lax.broadcasted_iota(jnp.int32, (128, 128), 0)
             <= lax.broadcasted_iota(jnp.int32, (128, 128), 1)).astype(F32)
    within = _dot_exact(logf, upper)
    totals = jnp.broadcast_to(within[:, 127:128], (rows, 128))
    strict_lower = (lax.broadcasted_iota(jnp.int32, (rows, rows), 1)
                    < lax.broadcasted_iota(jnp.int32, (rows, rows), 0)).astype(F32)
    o_ref[0] = (within + _dot_exact(strict_lower, totals)) * LOG2E


def forget_cumsum_call(fl_t, fox_fb):
    h, s = fl_t.shape
    rows = s // 128
    out = pl.pallas_call(
        _forget_cumsum_kernel,
        grid=(h,),
        in_specs=[pl.BlockSpec(memory_space=pltpu.SMEM),
                  pl.BlockSpec((1, rows, 128), lambda i: (i, 0, 0))],
        out_specs=pl.BlockSpec((1, rows, 128), lambda i: (i, 0, 0)),
        out_shape=jax.ShapeDtypeStruct((h, rows, 128), F32),
        compiler_params=_params(1),
        name="forget_cumsum",
    )(fox_fb.astype(F32), fl_t.reshape(h, rows, 128))
    return out.reshape(h, s)


def _online_softmax_step(s, offset, v_ones, m_ref, acc_ref, idx, guard_empty_rows=False):
    logits = s if callable(s) else (lambda which: s)
    m_old = m_ref[idx]
    m_new = jnp.maximum(m_old, jnp.max(logits(0), -1, keepdims=True) + offset)
    m_fin = jnp.where(m_new == NEG_INF, 0.0, m_new) if guard_empty_rows else m_new
    alpha = jnp.exp2(m_old - m_fin)
    s2 = logits(1)
    p = jnp.exp2(s2 - jnp.tile(m_fin - offset, (1, s2.shape[1] // LANES)))
    acc_ref[idx] = jnp.tile(alpha, (1, 2)) * acc_ref[idx] + _dot(p.astype(BF16), v_ones)
    m_ref[idx] = m_new


def _reread(buf, slot, which, zero_ref):
    if which == 0:
        return buf[slot]
    rows = buf.shape[1]
    return buf[slot, pl.ds(pl.multiple_of(zero_ref[0], rows), rows), :]


def _init_softmax_state(m_ref, acc_ref):
    m_ref[...] = jnp.full(m_ref.shape, NEG_INF, F32)
    acc_ref[...] = jnp.zeros(acc_ref.shape, F32)


def _with_ones(v):
    return jnp.concatenate([v, jnp.ones(v.shape, v.dtype)], axis=1)


def _normalised(acc):
    return acc[:, :HEAD_DIM] / acc[:, HEAD_DIM:]


def _tile_rows_cols(t):
    return (lax.broadcasted_iota(jnp.int32, (t, t), 0),
            lax.broadcasted_iota(jnp.int32, (t, t), 1))


def _sweep_tiles_pipelined(qi, has_near, scores_into, consume, buf0, buf1, unroll, far_step=None):
    bufs = (buf0, buf1)
    n_far = jnp.maximum(qi - 1, 0) if has_near else qi
    scores_into(0, buf0)

    def run(first, count):
        for u in range(count):
            if far_step is not None:
                far_step(first + u + 1, bufs[(u + 1) % 2], first + u, bufs[u % 2])
                continue
            scores_into(first + u + 1, bufs[(u + 1) % 2])
            consume(first + u, bufs[u % 2], "far")

    def group(j, carry):
        run(unroll * j, unroll)
        return carry

    lax.fori_loop(0, n_far // unroll, group, 0)
    rest = n_far % unroll
    step = unroll // 2
    while step >= 1:
        taken = rest - rest % (2 * step)

        @pl.when(rest % (2 * step) >= step)
        def _(first=n_far - rest + taken, count=step):
            run(first, count)

        step //= 2

    for parity in range(2):
        cur, nxt = bufs[parity], bufs[1 - parity]
        if has_near:
            @pl.when((qi >= 1) & (n_far % 2 == parity))
            def _(cur=cur, nxt=nxt):
                scores_into(qi, nxt)
                consume(qi - 1, cur, "near")
                consume(qi, nxt, "diag")
        else:
            @pl.when(n_far % 2 == parity)
            def _(cur=cur):
                consume(qi, cur, "diag")
    if has_near:
        @pl.when(qi == 0)
        def _():
            consume(qi, buf0, "diag")


def _causal_keep(t):
    return lax.broadcasted_iota(jnp.int32, (t, t), 1) <= lax.broadcasted_iota(jnp.int32, (t, t), 0)


def _key_tile(ref, ki, t):
    return ref[pl.ds(pl.multiple_of(ki * t, t), t), :]


def _fox_kernel(q_ref, k_ref, v_ref, ccol_ref, crow_ref, o_ref, m_ref, acc_ref, buf0, buf1, *, t):
    qi = pl.program_id(1)
    q = q_ref[...]
    cq = jnp.broadcast_to(ccol_ref[0], (t, LANES))
    _init_softmax_state(m_ref, acc_ref)

    def scores_into(ki, buf):
        buf[0] = _dot_nt(q, _key_tile(k_ref, ki, t))

    def consume(ki, buf, kind):
        c_keys = crow_ref[0, ki]
        s = buf[0] - c_keys
        if kind == "diag":
            s = jnp.where(_causal_keep(t), s, NEG_INF)
        _online_softmax_step(s, cq, _with_ones(_key_tile(v_ref, ki, t)), m_ref, acc_ref, 0)

    _sweep_tiles_pipelined(qi, False, scores_into, consume, buf0, buf1, SWEEP_UNROLL)
    o_ref[...] = _normalised(acc_ref[0]).astype(o_ref.dtype)


def fox_attention_call(zb, c, q_off, k_off, v_off, t):
    s = zb.shape[0]
    h = FOX_HEADS
    nt = s // t
    c_col = c.reshape(h, s, 1)
    c_row = c.reshape(h, nt, 1, t)
    return pl.pallas_call(
        functools.partial(_fox_kernel, t=t),
        grid=(h, nt),
        in_specs=[pl.BlockSpec((t, HEAD_DIM), lambda hh, i: (i, q_off + hh)),
                  pl.BlockSpec((s, HEAD_DIM), lambda hh, i: (0, k_off + hh)),
                  pl.BlockSpec((s, HEAD_DIM), lambda hh, i: (0, v_off + hh)),
                  pl.BlockSpec((1, t, 1), lambda hh, i: (hh, i, 0)),
                  pl.BlockSpec((1, nt, 1, t), lambda hh, i: (hh, 0, 0, 0))],
        out_specs=pl.BlockSpec((t, HEAD_DIM), lambda hh, i: (i, hh)),
        out_shape=jax.ShapeDtypeStruct((s, h * HEAD_DIM), BF16),
        scratch_shapes=[pltpu.VMEM((1, t, LANES), F32), pltpu.VMEM((1, t, 2 * HEAD_DIM), F32),
                        pltpu.VMEM((1, t, t), F32), pltpu.VMEM((1, t, t), F32)],
        compiler_params=_params(2),
        name="fox_attention",
    )(zb, zb, zb, c_col, c_row)


def _t5_bias(tbl_ref, head, dist):
    bias = jnp.full(dist.shape, tbl_ref[0, head], F32)
    for b in range(1, T5_BUCKETS):
        bias = jnp.where(dist >= T5_THRESH[b], tbl_ref[b, head], bias)
    return bias


def _fill_t5_tiles(tbl_ref, head, bias_ref, slot, t):
    rows, cols = _tile_rows_cols(t)
    dist = rows - cols
    bias_ref[slot, 0] = _t5_bias(tbl_ref, head, dist) * LOG2E
    bias_ref[slot, 1] = _t5_bias(tbl_ref, head, dist + t) * LOG2E


def _diff_kernel(tbl_ref, zero_ref, q_ref, k_ref, v_ref, lam_ref, g_ref, o_ref,
                 m_ref, acc_ref, bias_ref, buf0, buf1, *, t, lambda_init):
    head = pl.program_id(0)
    qi = pl.program_id(1)

    @pl.when(qi == 0)
    def _():
        _fill_t5_tiles(tbl_ref, head, bias_ref, 0, t)

    q = q_ref[...]
    lane = lax.broadcasted_iota(jnp.int32, q.shape, 1)
    zero = jnp.zeros_like(q)
    q_maps = (jnp.where(lane < HEAD_DIM // 2, q, zero), jnp.where(lane >= HEAD_DIM // 2, q, zero))
    _init_softmax_state(m_ref, acc_ref)
    far_bias = tbl_ref[T5_BUCKETS - 1, head] * LOG2E

    def scores_into(ki, buf):
        k = _key_tile(k_ref, ki, t)
        for mi in range(2):
            buf[mi] = _dot_nt(q_maps[mi], k)

    def consume(ki, buf, kind):
        v_ones = _with_ones(_key_tile(v_ref, ki, t))
        for mi in range(2):
            if kind == "far":
                _online_softmax_step(lambda which, mi=mi: _reread(buf, mi, which, zero_ref),
                                     far_bias, v_ones, m_ref, acc_ref, mi)
                continue
            s = buf[mi] + bias_ref[0, 1 if kind == "near" else 0]
            if kind == "diag":
                s = jnp.where(_causal_keep(t), s, NEG_INF)
            _online_softmax_step(s, 0.0, v_ones, m_ref, acc_ref, mi)

    _sweep_tiles_pipelined(qi, True, scores_into, consume, buf0, buf1, SWEEP_UNROLL)

    lam = lam_ref[...]
    lmbda = (jnp.exp(jnp.sum(lam[0:1] * lam[1:2], keepdims=True))
             - jnp.exp(jnp.sum(lam[2:3] * lam[3:4], keepdims=True)) + lambda_init)
    o = _normalised(acc_ref[0]) - lmbda * _normalised(acc_ref[1])
    y = o * lax.rsqrt(jnp.mean(o * o, -1, keepdims=True) + NORM_EPS) * g_ref[...]
    o_ref[...] = (y * (1.0 - lambda_init)).astype(o_ref.dtype)


def diff_attention_call(zb, t5_table, diff_lambda, subln, lambda_init, q_off, k_off, v_off, t):
    s = zb.shape[0]
    h = DIFF_HEADS
    nt = s // t
    return pl.pallas_call(
        functools.partial(_diff_kernel, t=t, lambda_init=lambda_init),
        grid=(h, nt),
        in_specs=[pl.BlockSpec(memory_space=pltpu.SMEM),
                  pl.BlockSpec(memory_space=pltpu.SMEM),
                  pl.BlockSpec((t, HEAD_DIM), lambda hh, i: (i, q_off + hh)),
                  pl.BlockSpec((s, HEAD_DIM), lambda hh, i: (0, k_off + hh)),
                  pl.BlockSpec((s, HEAD_DIM), lambda hh, i: (0, v_off + hh)),
                  pl.BlockSpec(diff_lambda.shape, lambda hh, i: (0, 0)),
                  pl.BlockSpec((1, HEAD_DIM), lambda hh, i: (0, 0))],
        out_specs=pl.BlockSpec((t, HEAD_DIM), lambda hh, i: (i, hh)),
        out_shape=jax.ShapeDtypeStruct((s, h * HEAD_DIM), BF16),
        scratch_shapes=[pltpu.VMEM((2, t, LANES), F32), pltpu.VMEM((2, t, 2 * HEAD_DIM), F32),
                        pltpu.VMEM((1, 2, t, t), F32),
                        pltpu.VMEM((2, t, t), F32), pltpu.VMEM((2, t, t), F32)],
        compiler_params=_params(2),
        name="diff_attention",
    )(t5_table.astype(F32), jnp.zeros((1,), jnp.int32), zb, zb, zb, diff_lambda.astype(F32),
      subln.reshape(1, -1).astype(F32))


def _compress_kernel(kr_ref, pos_ref, w1_ref, w2_ref, o_ref):
    half = w1_ref.shape[1] // 2
    kr = kr_ref[0, 0]
    w_top = w1_ref[0, :half, :]
    w_bot = w1_ref[0, half:, :]
    pos = pos_ref[0].astype(BF16)
    first = _dot(kr, w_top)
    second = _dot(kr, w_bot)
    pos_part = _dot(pos[:, :half], w_top) + _dot(pos[:, half:], w_bot)
    rows = kr.shape[0]
    second_next = pltpu.roll(second, rows - 1, 0)
    hid = first + second_next + pos_part[0:1, :]
    hid = hid * jax.nn.sigmoid(hid)
    o_ref[0, 0] = _dot(hid.astype(BF16), w2_ref[0]).astype(o_ref.dtype)


def compress_call(kv_rows, pos, w1, w2):
    two, g, nc, width = kv_rows.shape
    return pl.pallas_call(
        _compress_kernel,
        grid=(two, g),
        in_specs=[pl.BlockSpec((1, 1, nc, width), lambda a, b: (a, b, 0, 0)),
                  pl.BlockSpec((1, 16, 2 * width), lambda a, b: (a, 0, 0)),
                  pl.BlockSpec((1, 2 * width, HEAD_DIM), lambda a, b: (a, 0, 0)),
                  pl.BlockSpec((1, HEAD_DIM, HEAD_DIM), lambda a, b: (a, 0, 0))],
        out_specs=pl.BlockSpec((1, 1, nc, HEAD_DIM), lambda a, b: (a, b, 0, 0)),
        out_shape=jax.ShapeDtypeStruct((two, g, nc, HEAD_DIM), BF16),
        compiler_params=_params(2),
        name="nsa_compress",
    )(kv_rows, pos, w1, w2)


def _nsa_cmp_kernel(tbl_ref, q_ref, kc_ref, vc_ref, gate_ref, o_ref, sel_ref, qext_ref, *,
                    head0, qb0):
    step = pl.program_id(0)
    qb = qb0 + step
    tq = q_ref.shape[0]
    ncp = kc_ref.shape[2]
    ns = sel_ref.shape[2]
    n_heads = NSA_KV_HEADS * NSA_GROUP

    @pl.when(step == 0)
    def _():
        rows = lax.broadcasted_iota(jnp.int32, (tq, LANES), 0)
        lane = lax.broadcasted_iota(jnp.int32, (tq, LANES), 1)
        slot = lane % CMP_BAND
        piece = lane // CMP_BAND
        dist = rows - (slot - CMP_BAND_BACK) * CMP_STRIDE - (CMP_LEN - 1)
        for hd in range(n_heads):
            far = tbl_ref[T5_BUCKETS - 1, head0 + hd]
            delta = (_t5_bias(tbl_ref, head0 + hd, dist) - far) * LOG2E
            hi, mid, lo = (x.astype(F32) for x in _split_bf16(delta))
            packed = jnp.where(piece == 0, hi, jnp.where(piece == 1, mid, jnp.where(piece == 2, lo, 0.0)))
            qext_ref[hd] = packed.astype(BF16)

    t_pos = qb * tq + lax.broadcasted_iota(jnp.int32, (tq, ncp), 0)
    cmp_end = lax.broadcasted_iota(jnp.int32, (tq, ncp), 1) * CMP_STRIDE + (CMP_LEN - 1)
    visible = cmp_end <= t_pos

    first_tok = qb * (tq // CMP_STRIDE) - CMP_BAND_BACK
    tok = lax.broadcasted_iota(jnp.int32, (ncp, LANES), 0)
    klane = lax.broadcasted_iota(jnp.int32, (ncp, LANES), 1)
    in_slot = (klane < 3 * CMP_BAND) & (tok == first_tok + klane % CMP_BAND)
    kext = jnp.where(in_slot, 1.0, 0.0).astype(BF16)

    ci = lax.broadcasted_iota(jnp.int32, (ncp, ns), 0) * CMP_STRIDE
    bj = lax.broadcasted_iota(jnp.int32, (ncp, ns), 1) * SLC_LEN
    overlap = jnp.where((ci <= bj + SLC_LEN - 1) & (ci + CMP_LEN - 1 >= bj), 1.0, 0.0).astype(BF16)
    t_sel = qb * tq + lax.broadcasted_iota(jnp.int32, (tq, ns), 0)
    blk = lax.broadcasted_iota(jnp.int32, (tq, ns), 1)
    cur = t_sel // SLC_LEN
    forced = (blk == 0) | (blk == cur) | (blk == cur - 1)
    valid = blk * SLC_LEN <= t_sel

    scores = []
    for grp in range(NSA_KV_HEADS):
        k_aug = jnp.concatenate([kc_ref[0, grp], kext], axis=1)
        vc = vc_ref[0, grp]
        gates = jax.nn.sigmoid(gate_ref[0, grp])
        p_sum = jnp.zeros((tq, ncp), F32)
        for r in range(NSA_GROUP):
            hd = grp * NSA_GROUP + r
            cols = slice(hd * HEAD_DIM, (hd + 1) * HEAD_DIM)
            q_aug = jnp.concatenate([q_ref[:, cols], qext_ref[hd]], axis=1)
            far2 = tbl_ref[T5_BUCKETS - 1, head0 + hd] * LOG2E
            s = jnp.where(visible, _dot_nt(q_aug, k_aug), NEG_INF)
            m = jnp.max(s, -1, keepdims=True) + far2
            m = jnp.where(m == NEG_INF, 0.0, m)
            e = jnp.exp2(s - (m - far2))
            p = e * (1.0 / jnp.maximum(jnp.sum(e, -1, keepdims=True), 1e-30))
            p_sum = p_sum + p
            o = _dot(p.astype(BF16), vc) * gates[:, r:r + 1]
            o_ref[:, cols] = o.astype(o_ref.dtype)
        p_hi, p_mid, p_lo = _split_bf16(p_sum)
        imp = _dot(p_hi, overlap) + _dot(p_mid, overlap) + _dot(p_lo, overlap)
        scores.append(jnp.where(forced, SEL_BIG, jnp.where(valid, imp, -SEL_BIG)))

    score = jnp.concatenate(scores, axis=0)
    blk_f = jnp.concatenate([blk] * NSA_KV_HEADS, axis=0).astype(F32)

    def pick(_, carry):
        sc, chosen = carry
        best = jnp.max(sc, -1, keepdims=True)
        first = jnp.min(jnp.where(sc == best, blk_f, float(ns)), -1, keepdims=True)
        hit = blk_f == first
        return jnp.where(hit, NEG_INF, sc), jnp.where(hit, 1.0, chosen)

    _, chosen = lax.fori_loop(0, min(SLC_TOPN, ns), pick, (score, jnp.zeros(score.shape, F32)))
    for grp in range(NSA_KV_HEADS):
        sel_ref[grp] = chosen[grp * tq:(grp + 1) * tq].astype(sel_ref.dtype)


def nsa_cmp_call(zb, kvc, gates, t5_table, q_off, ns):
    s = zb.shape[0]
    g = NSA_KV_HEADS
    tq = CMP_QBLOCK
    ncp = kvc.shape[2]
    width = NSA_HEADS * HEAD_DIM
    n_ranges = next(r for r in (4, 2, 1) if ncp % (r * LANES) == 0)
    steps = s // tq // n_ranges
    outs = []
    for c in range(n_ranges):
        ncp_c = ncp * (c + 1) // n_ranges
        first = c * steps
        outs.append(pl.pallas_call(
            functools.partial(_nsa_cmp_kernel, head0=DIFF_HEADS, qb0=first),
            grid=(steps,),
            in_specs=[pl.BlockSpec(memory_space=pltpu.SMEM),
                      pl.BlockSpec((tq, width), lambda i, first=first: (first + i, q_off)),
                      pl.BlockSpec((1, g, ncp_c, HEAD_DIM), lambda i: (0, 0, 0, 0)),
                      pl.BlockSpec((1, g, ncp_c, HEAD_DIM), lambda i: (1, 0, 0, 0)),
                      pl.BlockSpec((1, g, tq, NSA_GROUP), lambda i, first=first: (0, 0, first + i, 0))],
            out_specs=[pl.BlockSpec((tq, width), lambda i: (i, 0)),
                       pl.BlockSpec((g, tq, ns), lambda i: (0, i, 0))],
            out_shape=[jax.ShapeDtypeStruct((steps * tq, width), BF16),
                       jax.ShapeDtypeStruct((g, steps * tq, ns), BF16)],
            scratch_shapes=[pltpu.VMEM((NSA_HEADS, tq, LANES), BF16)],
            compiler_params=_params(1),
            name="nsa_compressed",
        )(t5_table.astype(F32), zb, kvc, kvc, gates))
    o_cmp = jnp.concatenate([o for o, _ in outs], axis=0)
    sel = jnp.concatenate([m for _, m in outs], axis=1)
    return o_cmp, sel


def _nsa_sweep_kernel(tbl_ref, q_ref, k_ref, v_ref, gate_ref, *rest, t, mode, head0, branch):
    if mode == "selected":
        sel_ref, o_ref, m_ref, acc_ref, bias_ref, buf0, buf1 = rest
    else:
        o_ref, m_ref, acc_ref, bias_ref, buf0, buf1 = rest
    grp = pl.program_id(0)
    qi = pl.program_id(1)

    @pl.when(qi == 0)
    def _():
        for r in range(NSA_GROUP):
            _fill_t5_tiles(tbl_ref, head0 + grp * NSA_GROUP + r, bias_ref, r, t)

    _init_softmax_state(m_ref, acc_ref)

    all_heads = tuple(range(NSA_GROUP))

    def scores_into(ki, buf, heads=all_heads, with_mask=True):
        k = _key_tile(k_ref, ki, t)
        for r in heads:
            buf[r] = _dot_nt(q_ref[:, r * HEAD_DIM:(r + 1) * HEAD_DIM], k)
        if mode == "selected" and with_mask:
            ns = sel_ref.shape[2]
            blk_of_key = (ki * t + lax.broadcasted_iota(jnp.int32, (ns, t), 1)) // SLC_LEN
            expand = jnp.where(lax.broadcasted_iota(jnp.int32, (ns, t), 0) == blk_of_key, 1.0, 0.0)
            buf[NSA_GROUP] = _dot(sel_ref[0], expand.astype(BF16))

    def consume(ki, buf, kind, heads=all_heads):
        v_ones = _with_ones(_key_tile(v_ref, ki, t))
        if mode == "selected":
            keep = buf[NSA_GROUP] > 0.5
            if kind == "diag":
                keep = keep & _causal_keep(t)
        elif kind == "diag":
            keep = _causal_keep(t)
        else:
            keep = jnp.logical_not(_causal_keep(t))
        for r in heads:
            s = buf[r]
            if kind != "far":
                s = s + bias_ref[r, 1 if kind == "near" else 0]
            far_bias = tbl_ref[T5_BUCKETS - 1, head0 + grp * NSA_GROUP + r] * LOG2E
            _online_softmax_step(jnp.where(keep, s, NEG_INF), far_bias if kind == "far" else 0.0,
                                 v_ones, m_ref, acc_ref, r, guard_empty_rows=(mode == "window"))

    def far_step(next_ki, nxt, ki, cur):
        half = NSA_GROUP // 2
        for part in range(2):
            @pl.when(qi >= 0)
            def _(heads=all_heads[part * half:(part + 1) * half], with_mask=(part == 1)):
                scores_into(next_ki, nxt, heads, with_mask)
                consume(ki, cur, "far", heads)

    if mode == "selected":
        _sweep_tiles_pipelined(qi, True, scores_into, consume, buf0, buf1, SWEEP_UNROLL_4HEADS,
                               far_step=far_step)
    else:
        @pl.when(qi >= 1)
        def _():
            scores_into(qi - 1, buf0)
            consume(qi - 1, buf0, "near")

        scores_into(qi, buf1)
        consume(qi, buf1, "diag")

    gates = jax.nn.sigmoid(gate_ref[0, 0])
    for r in range(NSA_GROUP):
        o = _normalised(acc_ref[r]) * gates[:, r:r + 1]
        o_ref[:, r * HEAD_DIM:(r + 1) * HEAD_DIM] = o.astype(o_ref.dtype)


def nsa_sweep_call(zb, gates, t5_table, sel, q_off, k_off, v_off, t, mode):
    s = zb.shape[0]
    g = NSA_KV_HEADS
    nt = s // t
    width = NSA_GROUP * HEAD_DIM
    branch = 1 if mode == "selected" else 2
    n_buf = NSA_GROUP + 1 if mode == "selected" else NSA_GROUP
    if mode == "window":
        assert t == WINDOW
    in_specs = [pl.BlockSpec(memory_space=pltpu.SMEM),
                pl.BlockSpec((t, width), lambda gg, i: (i, q_off + gg)),
                pl.BlockSpec((s, HEAD_DIM), lambda gg, i: (0, k_off + gg)),
                pl.BlockSpec((s, HEAD_DIM), lambda gg, i: (0, v_off + gg)),
                pl.BlockSpec((1, 1, t, NSA_GROUP), lambda gg, i: (branch, gg, i, 0))]
    args = [t5_table.astype(F32), zb, zb, zb, gates]
    if mode == "selected":
        ns = sel.shape[2]
        in_specs.append(pl.BlockSpec((1, t, ns), lambda gg, i: (gg, i, 0)))
        args.append(sel)
    return pl.pallas_call(
        functools.partial(_nsa_sweep_kernel, t=t, mode=mode, head0=DIFF_HEADS, branch=branch),
        grid=(g, nt),
        in_specs=in_specs,
        out_specs=pl.BlockSpec((t, width), lambda gg, i: (i, gg)),
        out_shape=jax.ShapeDtypeStruct((s, g * width), BF16),
        scratch_shapes=[pltpu.VMEM((NSA_GROUP, t, LANES), F32),
                        pltpu.VMEM((NSA_GROUP, t, 2 * HEAD_DIM), F32),
                        pltpu.VMEM((NSA_GROUP, 2, t, t), F32),
                        pltpu.VMEM((n_buf, t, t), F32), pltpu.VMEM((n_buf, t, t), F32)],
        compiler_params=_params(2),
        name="nsa_" + mode,
    )(*args)


def _gate_columns(w_in, layer, start):
    w = w_in[layer, :, start:]
    return jnp.pad(w, ((0, 0), (0, LANES - w.shape[1])))


def _column_tile(n):
    return next(tn for tn in (1024, 1408, 512, 256, 128) if n % tn == 0)


def _query_column_scale(n, spans):
    cs = np.ones((1, n), np.float32)
    for start, stop, width in spans:
        cs[0, start:stop] = width ** -0.5 * LOG2E
    return jnp.asarray(cs)


def even_mixer(x, norm_g, w_in, layer, ret_gn, fox_fb, w_out, t):
    main = 7 * RET_HEADS * HEAD_DIM
    fox_q = 4 * RET_HEADS * HEAD_DIM
    col_scale = _query_column_scale(main, [(fox_q, fox_q + FOX_HEADS * HEAD_DIM, HEAD_DIM)])
    zb, zg = norm_proj_call(x, norm_g, w_in, layer, col_scale, _gate_columns(w_in, layer, main),
                            main, _column_tile(main))
    ret_out = retention_call(zb, ret_gn)
    c = forget_cumsum_call(zg[:, :FOX_HEADS].T, fox_fb)
    fox_out = fox_attention_call(zb, c, 4 * RET_HEADS, 5 * RET_HEADS, 6 * RET_HEADS, t)
    return residual_matmul_call(x, [ret_out], w_out, layer, [fox_out], tn=1024)


def odd_mixer(x, norm_g, w_in, layer, diff_lambda, diff_subln, cmp_pos, cmp_w1, cmp_w2, w_out,
              t5_table, lambda_init, t):
    s = x.shape[0]
    main = 5632
    hd = HEAD_DIM
    nsa_q = 3 * DIFF_HEADS * hd
    col_scale = _query_column_scale(main, [(0, DIFF_HEADS * hd, hd // 2),
                                           (nsa_q, nsa_q + NSA_HEADS * hd, hd)])
    zb, zg = norm_proj_call(x, norm_g, w_in, layer, col_scale, _gate_columns(w_in, layer, main),
                            main, _column_tile(main))
    diff_out = diff_attention_call(zb, t5_table, diff_lambda, diff_subln, lambda_init, 0, 8, 16, t)
    g = NSA_KV_HEADS
    ckv = zb[:, 32 * hd:36 * hd].reshape(s // CMP_STRIDE, CMP_STRIDE, 2, g, hd)
    kv_rows = ckv.transpose(2, 3, 0, 1, 4).reshape(2, g, s // CMP_STRIDE, CMP_STRIDE * hd)
    pos = jnp.broadcast_to(cmp_pos.reshape(2, 1, CMP_LEN * hd), (2, 16, CMP_LEN * hd)).astype(F32)
    kvc = compress_call(kv_rows, pos, cmp_w1.astype(BF16), cmp_w2.astype(BF16))
    gates = zg[:, :3 * NSA_HEADS].reshape(s, 3, g, NSA_GROUP).transpose(1, 2, 0, 3)
    ns = s // SLC_LEN
    o_cmp, sel = nsa_cmp_call(zb, kvc, gates, t5_table, 3, ns)
    o_slc = nsa_sweep_call(zb, gates, t5_table, sel, 6, 36, 38, t, "selected")
    o_win = nsa_sweep_call(zb, gates, t5_table, None, 6, 40, 42, WINDOW, "window")
    return residual_matmul_call(x, [diff_out], w_out, layer, [o_cmp, o_slc, o_win], tn=1024)


def ffn_and_ple(x, norm_g, w_gate, w_up, w_down, p, ple_gate, ple_proj, layer):
    u = norm_swiglu_call(x, norm_g, w_gate, w_up, layer)
    x2, x2b = residual_matmul_call(x, [u], w_down, layer, with_bf16=True)
    return ple_call(x2, x2b, ple_gate, p, ple_proj, layer, tn=1024)


def _attention_tile(s):
    return min(512, s)


def kernel(x, p, norm_mix, norm_ffn, w_in_even, ret_gn, fox_fb, w_out_even, w_in_odd, diff_lambda, diff_subln, cmp_pos, cmp_w1, cmp_w2, w_out_odd, t5_table, ffn_gate, ffn_up, ffn_down, ple_gate, ple_proj, final_norm):
    batch, s, d = x.shape
    assert batch == 1
    depth = p.shape[0]
    t = _attention_tile(s)
    xs = x[0]
    (w_in_even, w_out_even, w_in_odd, w_out_odd, ffn_gate, ffn_up, ffn_down, ple_gate, ple_proj) = (
        w.astype(BF16) for w in (w_in_even, w_out_even, w_in_odd, w_out_odd, ffn_gate, ffn_up,
                                 ffn_down, ple_gate, ple_proj))
    for i in range(depth):
        j = i // 2
        if i % 2 == 0:
            xs = even_mixer(xs, norm_mix[i], w_in_even, j, ret_gn[j], fox_fb[j], w_out_even, t)
        else:
            lambda_init = 0.8 - 0.6 * math.exp(-0.3 * i)
            xs = odd_mixer(xs, norm_mix[i], w_in_odd, j, diff_lambda[j], diff_subln[j], cmp_pos[j],
                           cmp_w1[j], cmp_w2[j], w_out_odd, t5_table, lambda_init, t)
        xs = ffn_and_ple(xs, norm_ffn[i], ffn_gate, ffn_up, ffn_down, p, ple_gate, ple_proj, i)
    out = rmsnorm_call(xs, final_norm, F32)
    return out[None]
```
